```python
import math
import jax
import jax.numpy as jnp
from jax import lax
import numpy as np

D_MODEL = 2048
BATCH = 8
SEQ = 4096
DEPTH = 4

CTX_LEN = 256
GRID_W = 64

RW_HEADS = 16
RW_HEAD = 64
RW_WIDTH = RW_HEADS * RW_HEAD
DECAY_LORA = 96
ICLR_LORA = 96
GATE_LORA = 64
RW_STREAM = 3 * RW_WIDTH + GATE_LORA + 2 * DECAY_LORA + 2 * ICLR_LORA
RW_GN_EPS = 64e-5

CONV_WIDTH = 1024
CONV_K = 31
EVEN_IN = RW_STREAM + 2 * CONV_WIDTH
EVEN_MIX = RW_WIDTH + CONV_WIDTH

ATT_HEADS = 16
ATT_KV_HEADS = 4
ATT_HEAD = 128
ATT_GROUP = ATT_HEADS // ATT_KV_HEADS
ATT_Q = ATT_HEADS * ATT_HEAD
ATT_KV = ATT_KV_HEADS * ATT_HEAD
ATT_IN = ATT_Q + 2 * ATT_KV
Q_BLOCK = 128
ROPE_THETA = 10000.0
ROPE_PAIRS = ATT_HEAD // 4
QK_EPS = 1e-6

N_EXPERTS = 64
TOP_K = 6
EXPERT_FF = 384
SHARED_FF = 384
ROUTE_SCALE = 2.5
EXPERT_BLOCK = 128

N_EVEN = (DEPTH + 1) // 2
N_ODD = DEPTH // 2
DEEPNORM_ALPHA = (2 * DEPTH) ** 0.25
DEEPNORM_BETA = (8 * DEPTH) ** -0.25
LN_EPS = 1e-5

kernel_name = 'hybrid_rwkv7_conformer_gqa_moe_dit'


def layer_norm(x, g, b):
    xf = x.astype(jnp.float32)
    mu = xf.mean(-1, keepdims=True)
    var = jnp.square(xf - mu).mean(-1, keepdims=True)
    return ((xf - mu) * lax.rsqrt(var + LN_EPS) * g + b).astype(x.dtype)


def rms_norm(x, g):
    xf = x.astype(jnp.float32)
    return (xf * lax.rsqrt(jnp.square(xf).mean(-1, keepdims=True) + QK_EPS) * g).astype(x.dtype)


def split_heads(t, n_heads, head_dim):
    return t.reshape(t.shape[:-1] + (n_heads, head_dim))


def axial_angles(n_tokens):
    rows_n = n_tokens // GRID_W
    row = jnp.repeat(jnp.arange(rows_n), GRID_W)
    col = jnp.tile(jnp.arange(GRID_W), rows_n)
    inv = ROPE_THETA ** (-jnp.arange(ROPE_PAIRS, dtype=jnp.float32) / ROPE_PAIRS)
    ang = jnp.stack([row[:, None] * inv, col[:, None] * inv], axis=1)
    return jnp.cos(ang), jnp.sin(ang)


def axial_rope(x, cos, sin):
    b_, s_, h_, _ = x.shape
    xr = x.astype(jnp.float32).reshape(b_, s_, h_, 2, 2, ROPE_PAIRS)
    x1, x2 = xr[..., 0, :], xr[..., 1, :]
    c = cos[None, :, None]
    s = sin[None, :, None]
    out = jnp.stack([x1 * c - x2 * s, x1 * s + x2 * c], axis=-2)
    return out.reshape(x.shape).astype(x.dtype)


def centred_shift(p, mu):
    prev = jnp.pad(p, ((0, 0), (1, 0), (0, 0)))[:, :-1]
    nxt = jnp.pad(p, ((0, 0), (0, 1), (0, 0)))[:, 1:]
    return p + mu[0] * (prev - p) + mu[1] * (nxt - p)


def rwkv_features(p, mu, w0, w_up, a0, a_up, g_up, k_k, k_a):
    p = centred_shift(p.astype(jnp.float32), mu)
    w_ = RW_WIDTH
    r, k, v = p[..., :w_], p[..., w_:2 * w_], p[..., 2 * w_:3 * w_]
    o = 3 * w_
    gd = p[..., o:o + GATE_LORA]
    o += GATE_LORA
    wd = p[..., o:o + 2 * DECAY_LORA].reshape(p.shape[:-1] + (2, DECAY_LORA))
    o += 2 * DECAY_LORA
    ad = p[..., o:o + 2 * ICLR_LORA].reshape(p.shape[:-1] + (2, ICLR_LORA))
    w_log = -jax.nn.softplus(-(w0 + jnp.einsum('btdl,dlc->btdc', jnp.tanh(wd), w_up))) - 0.5
    decay = jnp.exp(-jnp.exp(w_log))
    a = jax.nn.sigmoid(a0 + jnp.einsum('btdl,dlc->btdc', ad, a_up))
    g = jax.nn.sigmoid(gd) @ g_up
    kk = split_heads(k * k_k, RW_HEADS, RW_HEAD)
    kk = kk * lax.rsqrt(jnp.sum(jnp.square(kk), -1, keepdims=True) + 1e-12)
    k_dir = k[..., None, :] * (1.0 + (a - 1.0) * k_a)
    return r, v, g, kk, decay, a, k_dir


def wkv_scan(state0, r, w, k, v, kk, a, reverse):
    xs = tuple(jnp.swapaxes(t, 0, 1) for t in (r, w, k, v, kk, kk * a))

    def step(s, inp):
        r_t, w_t, k_t, v_t, kk_t, b_t = inp
        sk = jnp.einsum('bhvk,bhk->bhv', s, kk_t)
        s = s * w_t[:, :, None, :] - sk[..., None] * b_t[:, :, None, :] + v_t[..., None] * k_t[:, :, None, :]
        return s, jnp.einsum('bhvk,bhk->bhv', s, r_t)

    s_last, ys = lax.scan(step, state0, xs, reverse=reverse)
    return s_last, jnp.swapaxes(ys, 0, 1)


def rwkv_bidir(feats, states0, r_k, gn_g, gn_b):
    r, v, g, kk, decay, a, k_dir = feats
    rh = split_heads(r, RW_HEADS, RW_HEAD)
    vh = split_heads(v, RW_HEADS, RW_HEAD)
    ys, bonuses, finals = [], [], []
    for d in range(2):
        kd = split_heads(k_dir[:, :, d], RW_HEADS, RW_HEAD)
        s_fin, y_d = wkv_scan(states0[d], rh, split_heads(decay[:, :, d], RW_HEADS, RW_HEAD), kd, vh, kk,
                              split_heads(a[:, :, d], RW_HEADS, RW_HEAD), reverse=(d == 1))
        ys.append(y_d)
        finals.append(s_fin)
        bonuses.append(jnp.sum(rh * kd * r_k, -1, keepdims=True) * vh)
    y = ys[0] + ys[1]
    mean = y.mean(-1, keepdims=True)
    var = jnp.square(y - mean).mean(-1, keepdims=True)
    y = ((y - mean) * lax.rsqrt(var + RW_GN_EPS)).reshape(r.shape) * gn_g + gn_b
    out = (y + (bonuses[0] + bonuses[1]).reshape(r.shape)) * g
    return out, finals


def rwkv_time_mix(p_lat, p_ctx, mu, w0, w_up, a0, a_up, g_up, k_k, k_a, r_k, gn_g, gn_b):
    feats_ctx = rwkv_features(p_ctx, mu, w0, w_up, a0, a_up, g_up, k_k, k_a)
    feats_lat = rwkv_features(p_lat, mu, w0, w_up, a0, a_up, g_up, k_k, k_a)
    zero = jnp.zeros((p_lat.shape[0], RW_HEADS, RW_HEAD, RW_HEAD), jnp.float32)
    out_ctx, states_ctx = rwkv_bidir(feats_ctx, (zero, zero), r_k, gn_g, gn_b)
    out_lat, _ = rwkv_bidir(feats_lat, states_ctx, r_k, gn_g, gn_b)
    return out_lat, out_ctx


def conformer_conv(p, w, b, g, beta):
    val, gate = jnp.split(p, 2, axis=-1)
    z = val * jax.nn.sigmoid(gate)
    z = lax.conv_general_dilated(z, w[:, None, :].astype(z.dtype), (1,), [(CONV_K // 2, CONV_K // 2)],
                                 dimension_numbers=('NWC', 'WIO', 'NWC'),
                                 feature_group_count=CONV_WIDTH) + b
    return jax.nn.silu(layer_norm(z, g, beta))


def even_mixer(u, uc, w_in, w_out, mu, w0, w_up, a0, a_up, g_up, k_k, k_a, r_k, gn_g, gn_b,
               cv_w, cv_b, cv_g, cv_beta):
    p = u @ w_in
    pc = uc @ w_in
    rw_l, rw_c = rwkv_time_mix(p[..., :RW_STREAM], pc[..., :RW_STREAM], mu, w0, w_up, a0, a_up, g_up,
                               k_k, k_a, r_k, gn_g, gn_b)
    cv_l = conformer_conv(p[..., RW_STREAM:], cv_w, cv_b, cv_g, cv_beta)
    cv_c = conformer_conv(pc[..., RW_STREAM:], cv_w, cv_b, cv_g, cv_beta)
    y = jnp.concatenate([rw_l.astype(u.dtype), cv_l.astype(u.dtype)], -1) @ w_out
    yc = jnp.concatenate([rw_c.astype(u.dtype), cv_c.astype(u.dtype)], -1) @ w_out
    return y, yc


def gqa_attend(q, k, v):
    s = jnp.einsum('bqkgd,bskd->bkgqs', q, k).astype(jnp.float32) * (1.0 / math.sqrt(ATT_HEAD))
    p = jax.nn.softmax(s, axis=-1).astype(v.dtype)
    return jnp.einsum('bkgqs,bskd->bqkgd', p, v)


def attn_mixer(u, uc, w_in, w_out, qn, kn, cos, sin, with_ctx):
    def qkv(t):
        b_, t_ = t.shape[:2]
        p = t @ w_in
        q = rms_norm(p[..., :ATT_Q].reshape(b_, t_, ATT_HEADS, ATT_HEAD), qn)
        k = rms_norm(p[..., ATT_Q:ATT_Q + ATT_KV].reshape(b_, t_, ATT_KV_HEADS, ATT_HEAD), kn)
        v = p[..., ATT_Q + ATT_KV:].reshape(b_, t_, ATT_KV_HEADS, ATT_HEAD)
        return q, k, v

    q, k, v = qkv(u)
    qc, kc, vc = qkv(uc)
    q = axial_rope(q, cos, sin)
    k = axial_rope(k, cos, sin)
    keys = jnp.concatenate([k, kc], axis=1)
    vals = jnp.concatenate([v, vc], axis=1)
    b_, s_ = u.shape[:2]
    nb = s_ // Q_BLOCK
    qb = q.reshape(b_, nb, Q_BLOCK, ATT_KV_HEADS, ATT_GROUP, ATT_HEAD).swapaxes(0, 1)
    ob = lax.map(lambda qblk: gqa_attend(qblk, keys, vals), qb)
    y = ob.swapaxes(0, 1).reshape(b_, s_, ATT_Q) @ w_out
    yc = None
    if with_ctx:
        tc = qc.shape[1]
        qcg = qc.reshape(b_, tc, ATT_KV_HEADS, ATT_GROUP, ATT_HEAD)
        yc = gqa_attend(qcg, kc, vc).reshape(b_, tc, ATT_Q) @ w_out
    return y, yc


def swiglu(x, w1, w3, w2):
    return (jax.nn.silu(x @ w1) * (x @ w3)) @ w2


def moe(xf, router, bias, w1, w3, w2, sw1, sw3, sw2):
    n_tok = xf.shape[0]
    scores = jax.nn.sigmoid((xf @ router).astype(jnp.float32))
    _, idx = lax.top_k(scores + bias, TOP_K)
    gate = jnp.take_along_axis(scores, idx, axis=1)
    gate = ROUTE_SCALE * gate / jnp.sum(gate, -1, keepdims=True)
    n_assign = n_tok * TOP_K
    e_flat = idx.reshape(-1)
    order = jnp.argsort(e_flat)
    e_sorted = e_flat[order]
    tok_sorted = (order // TOP_K).astype(jnp.int32)
    g_sorted = gate.reshape(-1)[order]
    counts = jnp.zeros((N_EXPERTS,), jnp.int32).at[e_flat].add(1)
    padded = (counts + EXPERT_BLOCK - 1) // EXPERT_BLOCK * EXPERT_BLOCK
    start = jnp.cumsum(counts) - counts
    pend = jnp.cumsum(padded)
    pstart = pend - padded
    dest = pstart[e_sorted] + (jnp.arange(n_assign, dtype=jnp.int32) - start[e_sorted])
    n_blocks = (n_assign + N_EXPERTS * (EXPERT_BLOCK - 1) + EXPERT_BLOCK - 1) // EXPERT_BLOCK
    n_rows = n_blocks * EXPERT_BLOCK
    buf_tok = jnp.zeros((n_rows,), jnp.int32).at[dest].set(tok_sorted)
    buf_g = jnp.zeros((n_rows,), jnp.float32).at[dest].set(g_sorted)
    blk_pos = jnp.arange(n_blocks, dtype=jnp.int32) * EXPERT_BLOCK
    blk_e = jnp.minimum(jnp.searchsorted(pend, blk_pos, side='right'), N_EXPERTS - 1)

    def step(acc, inp):
        tok, g_b, e = inp
        xb = xf[tok]
        yb = swiglu(xb, w1[e], w3[e], w2[e]).astype(jnp.float32)
        return acc.at[tok].add(g_b[:, None] * yb), None

    routed, _ = lax.scan(step, jnp.zeros(xf.shape, jnp.float32),
                         (buf_tok.reshape(n_blocks, EXPERT_BLOCK), buf_g.reshape(n_blocks, EXPERT_BLOCK), blk_e))
    return swiglu(xf, sw1, sw3, sw2) + routed.astype(xf.dtype)


def setup_inputs(seed: int = 0) -> dict:
    key = jax.random.key(seed)
    ks = iter(jax.random.split(key, 48))
    D = D_MODEL

    def nrm(shape, scale):
        return jax.random.normal(next(ks), shape, jnp.float32) * scale

    def unif(shape, lo, hi):
        return jax.random.uniform(next(ks), shape, jnp.float32, minval=lo, maxval=hi)

    return {
        'x': nrm((BATCH, SEQ, D), 1.0),
        'c': nrm((BATCH, D), 1.0),
        'ctx': nrm((BATCH, CTX_LEN, D), 1.0),
        'c_ctx': nrm((D,), 1.0),
        'ada_w': nrm((DEPTH, D, 6 * D), 0.5 * D ** -0.5),
        'ada_b': nrm((DEPTH, 6 * D), 0.02),
        'ln1_g': 1.0 + nrm((DEPTH, D), 0.02),
        'ln1_b': nrm((DEPTH, D), 0.02),
        'ln2_g': 1.0 + nrm((DEPTH, D), 0.02),
        'ln2_b': nrm((DEPTH, D), 0.02),
        'even_w_in': nrm((N_EVEN, D, EVEN_IN), D ** -0.5),
        'even_w_out': nrm((N_EVEN, EVEN_MIX, D), DEEPNORM_BETA * EVEN_MIX ** -0.5),
        'rw_mu': unif((N_EVEN, 2, RW_STREAM), 0.0, 0.5),
        'rw_w0': unif((N_EVEN, 2, RW_WIDTH), -6.0, -0.5),
        'rw_w_up': nrm((N_EVEN, 2, DECAY_LORA, RW_WIDTH), 0.5 * DECAY_LORA ** -0.5),
        'rw_a0': nrm((N_EVEN, 2, RW_WIDTH), 0.5),
        'rw_a_up': nrm((N_EVEN, 2, ICLR_LORA, RW_WIDTH), 0.5 * ICLR_LORA ** -0.5),
        'rw_g_up': nrm((N_EVEN, GATE_LORA, RW_WIDTH), GATE_LORA ** -0.5),
        'rw_kk': 0.85 + nrm((N_EVEN, RW_WIDTH), 0.05),
        'rw_ka': 1.0 + nrm((N_EVEN, RW_WIDTH), 0.05),
        'rw_rk': nrm((N_EVEN, RW_HEADS, RW_HEAD), 0.1),
        'rw_gn_g': 1.0 + nrm((N_EVEN, RW_WIDTH), 0.02),
        'rw_gn_b': nrm((N_EVEN, RW_WIDTH), 0.02),
        'cv_w': nrm((N_EVEN, CONV_K, CONV_WIDTH), CONV_K ** -0.5),
        'cv_b': nrm((N_EVEN, CONV_WIDTH), 0.02),
        'cv_ln_g': 1.0 + nrm((N_EVEN, CONV_WIDTH), 0.02),
        'cv_ln_b': nrm((N_EVEN, CONV_WIDTH), 0.02),
        'odd_w_in': nrm((N_ODD, D, ATT_IN), D ** -0.5),
        'odd_w_out': nrm((N_ODD, ATT_Q, D), DEEPNORM_BETA * ATT_Q ** -0.5),
        'q_norm': 1.0 + nrm((N_ODD, ATT_HEAD), 0.02),
        'k_norm': 1.0 + nrm((N_ODD, ATT_HEAD), 0.02),
        'moe_router': nrm((DEPTH, D, N_EXPERTS), D ** -0.5),
        'moe_bias': nrm((DEPTH, N_EXPERTS), 0.01),
        'moe_w1': nrm((DEPTH, N_EXPERTS, D, EXPERT_FF), D ** -0.5),
        'moe_w3': nrm((DEPTH, N_EXPERTS, D, EXPERT_FF), D ** -0.5),
        'moe_w2': nrm((DEPTH, N_EXPERTS, EXPERT_FF, D), DEEPNORM_BETA * EXPERT_FF ** -0.5),
        'sh_w1': nrm((DEPTH, D, SHARED_FF), D ** -0.5),
        'sh_w3': nrm((DEPTH, D, SHARED_FF), D ** -0.5),
        'sh_w2': nrm((DEPTH, SHARED_FF, D), DEEPNORM_BETA * SHARED_FF ** -0.5),
    }


def reference(x, c, ctx, c_ctx, ada_w, ada_b, ln1_g, ln1_b, ln2_g, ln2_b, even_w_in, even_w_out,
              rw_mu, rw_w0, rw_w_up, rw_a0, rw_a_up, rw_g_up, rw_kk, rw_ka, rw_rk, rw_gn_g, rw_gn_b,
              cv_w, cv_b, cv_ln_g, cv_ln_b, odd_w_in, odd_w_out, q_norm, k_norm,
              moe_router, moe_bias, moe_w1, moe_w3, moe_w2, sh_w1, sh_w3, sh_w2):
    bsz, s_len, d = x.shape
    n_lat = bsz * s_len
    cos, sin = axial_angles(s_len)
    silu_c = jax.nn.silu(c)
    silu_cc = jax.nn.silu(c_ctx)
    h, hc = x, ctx
    for layer in range(DEPTH):
        last = layer == DEPTH - 1
        j = layer // 2
        mod = (silu_c @ ada_w[layer] + ada_b[layer])[:, None, :]
        mod_c = silu_cc @ ada_w[layer] + ada_b[layer]
        sh_a, sc_a, g_a, sh_f, sc_f, g_f = jnp.split(mod, 6, axis=-1)
        csh_a, csc_a, cg_a, csh_f, csc_f, cg_f = jnp.split(mod_c, 6, axis=-1)
        u = h * (1.0 + sc_a) + sh_a
        uc = hc * (1.0 + csc_a) + csh_a
        if layer % 2 == 0:
            y, yc = even_mixer(u, uc, even_w_in[j], even_w_out[j], rw_mu[j], rw_w0[j], rw_w_up[j], rw_a0[j],
                               rw_a_up[j], rw_g_up[j], rw_kk[j], rw_ka[j], rw_rk[j], rw_gn_g[j], rw_gn_b[j],
                               cv_w[j], cv_b[j], cv_ln_g[j], cv_ln_b[j])
        else:
            y, yc = attn_mixer(u, uc, odd_w_in[j], odd_w_out[j], q_norm[j], k_norm[j], cos, sin,
                               with_ctx=not last)
        h = layer_norm(DEEPNORM_ALPHA * h + g_a * y, ln1_g[layer], ln1_b[layer])
        vf = h * (1.0 + sc_f) + sh_f
        if last:
            f = moe(vf.reshape(-1, d), moe_router[layer], moe_bias[layer], moe_w1[layer], moe_w3[layer],
                    moe_w2[layer], sh_w1[layer], sh_w3[layer], sh_w2[layer])
            h = layer_norm(DEEPNORM_ALPHA * h + g_f * f.reshape(h.shape), ln2_g[layer], ln2_b[layer])
        else:
            hc = layer_norm(DEEPNORM_ALPHA * hc + cg_a * yc, ln1_g[layer], ln1_b[layer])
            vfc = hc * (1.0 + csc_f) + csh_f
            tokens = jnp.concatenate([vf.reshape(-1, d), vfc.reshape(-1, d)], axis=0)
            f = moe(tokens, moe_router[layer], moe_bias[layer], moe_w1[layer], moe_w3[layer], moe_w2[layer],
                    sh_w1[layer], sh_w3[layer], sh_w2[layer])
            h = layer_norm(DEEPNORM_ALPHA * h + g_f * f[:n_lat].reshape(h.shape), ln2_g[layer], ln2_b[layer])
            hc = layer_norm(DEEPNORM_ALPHA * hc + cg_f * f[n_lat:].reshape(hc.shape), ln2_g[layer], ln2_b[layer])
    return h
```

```python
import functools
import math

import jax
import jax.numpy as jnp
from jax import lax
from jax.experimental import pallas as pl
from jax.experimental.pallas import tpu as pltpu

F32 = jnp.float32
BF16 = jnp.bfloat16
U32 = jnp.uint32

D_MODEL = 2048
DEPTH = 4
GRID_W = 64

RW_HEADS = 16
RW_HEAD = 64
RW_WIDTH = RW_HEADS * RW_HEAD
DECAY_LORA = 96
ICLR_LORA = 96
GATE_LORA = 64
RW_GN_EPS = 64e-5
CONV_WIDTH = 1024
CONV_K = 31

ATT_HEADS = 16
ATT_KV_HEADS = 4
ATT_HEAD = 128
ATT_GROUP = ATT_HEADS // ATT_KV_HEADS
ATT_Q = ATT_HEADS * ATT_HEAD
ATT_KV = ATT_KV_HEADS * ATT_HEAD
ROPE_THETA = 10000.0
ROPE_PAIRS = ATT_HEAD // 4
QK_EPS = 1e-6

N_EXPERTS = 64
TOP_K = 6
EXPERT_FF = 384
ROUTE_SCALE = 2.5

DEEPNORM_ALPHA = (2 * DEPTH) ** 0.25
LN_EPS = 1e-5

LANES = 128
SMALL_W = 640
RW_STREAM_PAD = 3 * RW_WIDTH + SMALL_W
EVEN_IN_PAD = 6144
CONV_COL_BLOCK = 2
EXP_M05 = math.exp(-0.5)
NEG_BIG = -1e30
VMEM_LIMIT = 56 * 1024 * 1024

ROW_TILE = 256
HALO = 16
SCAN_STEPS = 32
MOE_BLOCK = 256
ROUTER_TILE = 512
DISPATCH_TILE = 256
COMBINE_TILE = 128


def _cparams(*sem):
    return pltpu.CompilerParams(dimension_semantics=tuple(sem), vmem_limit_bytes=VMEM_LIMIT)


def _layer_norm(x, g, b):
    mu = jnp.mean(x, axis=-1, keepdims=True)
    xc = x - mu
    var = jnp.mean(xc * xc, axis=-1, keepdims=True)
    return xc * lax.rsqrt(var + LN_EPS) * g + b


def _sigmoid(x):
    return 1.0 / (1.0 + jnp.exp(-x))


def _pack_pair(x):
    k = x.shape[1] // 2
    lo = lax.bitcast_convert_type(x[:, :k].astype(BF16).astype(F32), U32) >> 16
    hi = lax.bitcast_convert_type(x[:, k:].astype(BF16).astype(F32), U32) & jnp.uint32(0xFFFF0000)
    return hi | lo


def _unpack_pair(w):
    lo = lax.bitcast_convert_type(w << 16, F32)
    hi = lax.bitcast_convert_type(w & jnp.uint32(0xFFFF0000), F32)
    return jnp.concatenate([lo, hi], axis=1).astype(BF16)


def _col_tile(n, cap=1280):
    best = LANES
    for t in range(LANES, cap + 1, LANES):
        if n % t == 0:
            best = t
    return best


def _ada_kernel(c_ref, w_ref, b_ref, o_ref):
    c = c_ref[...]
    s = (c * _sigmoid(c)).astype(BF16)
    o_ref[0] = jnp.dot(s, w_ref[0].astype(BF16), preferred_element_type=F32) + b_ref[0]


def _ada_all(c16, ada_w, ada_b):
    depth, d, n6 = ada_w.shape
    tn = 1024
    return pl.pallas_call(
        _ada_kernel,
        out_shape=jax.ShapeDtypeStruct((depth, 16, n6), F32),
        grid=(depth, n6 // tn),
        in_specs=[
            pl.BlockSpec((16, d), lambda l, j: (0, 0)),
            pl.BlockSpec((1, d, tn), lambda l, j: (l, 0, j)),
            pl.BlockSpec((1, 1, tn), lambda l, j: (l, 0, j)),
        ],
        out_specs=pl.BlockSpec((1, 16, tn), lambda l, j: (l, 0, j)),
        compiler_params=_cparams("parallel", "parallel"),
        name="ada_mod",
    )(c16, ada_w, ada_b.reshape(depth, 1, n6))


def _inproj_kernel(x_ref, sc_ref, sh_ref, w_ref, o_ref, u_ref):
    @pl.when(pl.program_id(1) == 0)
    def _():
        u_ref[...] = (x_ref[...] * (1.0 + sc_ref[0]) + sh_ref[0]).astype(BF16)

    o_ref[...] = jnp.dot(u_ref[...], w_ref[...], preferred_element_type=F32).astype(o_ref.dtype)


def _inproj(h, mod3, w, dims, sc_idx, sh_idx):
    n, d = h.shape
    nout = w.shape[1]
    tm = min(512, dims[1])
    tn = _col_tile(nout)
    seq = _seq_of_tile(dims, tm)
    return pl.pallas_call(
        _inproj_kernel,
        out_shape=jax.ShapeDtypeStruct((n, nout), BF16),
        grid=(n // tm, nout // tn),
        in_specs=[
            pl.BlockSpec((tm, d), lambda i, j: (i, 0)),
            pl.BlockSpec((1, 1, d), lambda i, j: (seq(i) * 6 + sc_idx, 0, 0)),
            pl.BlockSpec((1, 1, d), lambda i, j: (seq(i) * 6 + sh_idx, 0, 0)),
            pl.BlockSpec((d, tn), lambda i, j: (0, j)),
        ],
        out_specs=pl.BlockSpec((tm, tn), lambda i, j: (i, j)),
        scratch_shapes=[pltpu.VMEM((tm, d), BF16)],
        compiler_params=_cparams("parallel", "arbitrary"),
        name="inproj",
    )(h, mod3, mod3, w)


def _seq_of_tile(dims, tm):
    b, s, _ = dims

    def seq(i):
        return jnp.minimum((i * tm) // s, b)

    return seq


def _outproj_kernel(*refs, splits):
    n_in = len(splits)
    a_refs = refs[:n_in]
    w_ref, h_ref, ga_ref, scf_ref, shf_ref, g_ref, b_ref, ho_ref, vf_ref = refs[n_in:]
    acc = None
    off = 0
    for a_ref, k in zip(a_refs, splits):
        part = jnp.dot(a_ref[...], w_ref[off:off + k, :], preferred_element_type=F32)
        acc = part if acc is None else acc + part
        off += k
    hn = _layer_norm(DEEPNORM_ALPHA * h_ref[...] + ga_ref[0] * acc, g_ref[...], b_ref[...])
    ho_ref[...] = hn
    vf_ref[...] = _pack_pair(hn * (1.0 + scf_ref[0]) + shf_ref[0])


def _outproj(acts, w, h, mod3, ln_g, ln_b, dims):
    n, d = h.shape
    tm = min(256, dims[1])
    seq = _seq_of_tile(dims, tm)
    splits = tuple(a.shape[1] for a in acts)
    in_specs = [pl.BlockSpec((tm, k), lambda i: (i, 0)) for k in splits]
    in_specs += [
        pl.BlockSpec(w.shape, lambda i: (0, 0)),
        pl.BlockSpec((tm, d), lambda i: (i, 0)),
        pl.BlockSpec((1, 1, d), lambda i: (seq(i) * 6 + 2, 0, 0)),
        pl.BlockSpec((1, 1, d), lambda i: (seq(i) * 6 + 4, 0, 0)),
        pl.BlockSpec((1, 1, d), lambda i: (seq(i) * 6 + 3, 0, 0)),
        pl.BlockSpec((1, d), lambda i: (0, 0)),
        pl.BlockSpec((1, d), lambda i: (0, 0)),
    ]
    return pl.pallas_call(
        functools.partial(_outproj_kernel, splits=splits),
        out_shape=(jax.ShapeDtypeStruct((n, d), F32), jax.ShapeDtypeStruct((n, d // 2), U32)),
        grid=(n // tm,),
        in_specs=in_specs,
        out_specs=(pl.BlockSpec((tm, d), lambda i: (i, 0)), pl.BlockSpec((tm, d // 2), lambda i: (i, 0))),
        compiler_params=_cparams("parallel"),
        name="outproj_ln",
    )(*acts, w, h, mod3, mod3, mod3, ln_g.reshape(1, d), ln_b.reshape(1, d))


def _tile_flags(dims, tb):
    b, s, ctx = dims
    n = b * (s + ctx)
    nlt = (b * s) // tb
    nbl = s // tb
    nbc = ctx // tb

    def first_last(i):
        il = i % nbl
        ic = (i - nlt) % nbc
        lat = i < nlt
        first = jnp.where(lat, il == 0, ic == 0)
        last = jnp.where(lat, il == nbl - 1, ic == nbc - 1)
        return first, last

    def prev_blk(i):
        return jnp.maximum((i * tb) // HALO - 1, 0)

    def next_blk(i):
        return jnp.minimum(((i + 1) * tb) // HALO, n // HALO - 1)

    return first_last, prev_blk, next_blk


def _head_sum(x, hs_ref):
    hi = x.astype(BF16)
    lo = (x - hi.astype(F32)).astype(BF16)
    hs = hs_ref[...]
    return jnp.dot(hi, hs, preferred_element_type=F32) + jnp.dot(lo, hs, preferred_element_type=F32)


def _feat_kernel(p_ref, pp_ref, pn_ref, mu_ref, w0_ref, a0_ref, wup_ref, aup_ref, gup_ref, kk_ref, ka_ref,
                 rk_ref, hs_ref, r_o, v_o, kkn_o, g_o, bon_o, dec_o, b_o, kd_o, buf_ref, *, tb, first_last):
    first, last = first_last(pl.program_id(0))
    w = RW_WIDTH
    buf_ref[8:8 + tb, :] = p_ref[...].astype(F32)
    buf_ref[7:8, :] = jnp.where(first, 0.0, pp_ref[HALO - 1:HALO, :].astype(F32))
    buf_ref[8 + tb:9 + tb, :] = jnp.where(last, 0.0, pn_ref[0:1, :].astype(F32))

    def shifted(c0, c1):
        cur = buf_ref[8:8 + tb, c0:c1]
        prev = buf_ref[7:7 + tb, c0:c1]
        nxt = buf_ref[9:9 + tb, c0:c1]
        return cur + mu_ref[0:1, c0:c1] * (prev - cur) + mu_ref[1:2, c0:c1] * (nxt - cur)

    r = shifted(0, w)
    k = shifted(w, 2 * w)
    v = shifted(2 * w, 3 * w)
    sm = shifted(3 * w, 3 * w + SMALL_W)

    g = jnp.dot(_sigmoid(sm[:, 0:LANES]).astype(BF16), gup_ref[...], preferred_element_type=F32)
    kkr = k * kk_ref[...]
    kk = kkr * lax.rsqrt(_head_sum(kkr * kkr, hs_ref) + 1e-12)

    ksum = None
    for d in range(2):
        wd = sm[:, LANES * (1 + d):LANES * (2 + d)]
        ad = sm[:, LANES * (3 + d):LANES * (4 + d)]
        xw = w0_ref[d:d + 1, :] + jnp.dot(jnp.tanh(wd).astype(BF16), wup_ref[d], preferred_element_type=F32)
        dec_o[d] = jnp.exp(-EXP_M05 * _sigmoid(xw))
        a = _sigmoid(a0_ref[d:d + 1, :] + jnp.dot(ad.astype(BF16), aup_ref[d], preferred_element_type=F32))
        b_o[d] = (a * kk).astype(BF16)
        kd = k * (1.0 + (a - 1.0) * ka_ref[...])
        kd_o[d] = kd.astype(BF16)
        ksum = kd if ksum is None else ksum + kd

    bon_o[...] = (_head_sum(r * ksum * rk_ref[...], hs_ref) * v).astype(BF16)
    r_o[...] = r.astype(BF16)
    v_o[...] = v.astype(BF16)
    kkn_o[...] = kk.astype(BF16)
    g_o[...] = g.astype(BF16)


def _rwkv_features(p, prm, dims):
    n = p.shape[0]
    tb = min(ROW_TILE, dims[2])
    first_last, prev_blk, next_blk = _tile_flags(dims, tb)
    w = RW_WIDTH
    pw = RW_STREAM_PAD
    full2 = lambda shape: pl.BlockSpec(shape, lambda i: (0, 0))
    full3 = lambda shape: pl.BlockSpec(shape, lambda i: (0, 0, 0))
    row = pl.BlockSpec((tb, w), lambda i: (i, 0))
    row2 = pl.BlockSpec((2, tb, w), lambda i: (0, i, 0))
    sd = lambda dt: jax.ShapeDtypeStruct((n, w), dt)
    sd2 = lambda dt: jax.ShapeDtypeStruct((2, n, w), dt)
    return pl.pallas_call(
        functools.partial(_feat_kernel, tb=tb, first_last=first_last),
        out_shape=(sd(BF16), sd(BF16), sd(BF16), sd(BF16), sd(BF16), sd2(F32), sd2(BF16), sd2(BF16)),
        grid=(n // tb,),
        in_specs=[
            pl.BlockSpec((tb, pw), lambda i: (i, 0)),
            pl.BlockSpec((HALO, pw), lambda i: (prev_blk(i), 0)),
            pl.BlockSpec((HALO, pw), lambda i: (next_blk(i), 0)),
            full2((2, pw)), full2((2, w)), full2((2, w)),
            full3((2, LANES, w)), full3((2, LANES, w)), full2((LANES, w)),
            full2((1, w)), full2((1, w)), full2((1, w)), full2((w, w)),
        ],
        out_specs=(row, row, row, row, row, row2, row2, row2),
        scratch_shapes=[pltpu.VMEM((tb + 16, pw), F32)],
        compiler_params=_cparams("parallel"),
        name="rwkv_features",
    )(p, p, p, prm["mu"], prm["w0"], prm["a0"], prm["w_up"], prm["a_up"], prm["g_up"], prm["kk"], prm["ka"],
      prm["rk"], prm["hs"])


def _wkv_kernel(r_ref, kk_ref, v_ref, w_ref, b_ref, kd_ref, y_ref, s_ref, *, steps):
    d = pl.program_id(0)

    @pl.when(pl.program_id(1) == 0)
    def _():
        s_ref[...] = jnp.zeros_like(s_ref)

    nk = RW_HEAD

    def step(i, carry):
        t = jnp.where(d == 0, i, steps - 1 - i)
        vt = v_ref[t]
        sk = jnp.zeros((nk, LANES), F32)
        for k in range(nk):
            sk = sk + s_ref[k] * kk_ref[t, pl.ds(k, 1), :]
        y = jnp.zeros((nk, LANES), F32)
        for k in range(nk):
            s = (s_ref[k] * w_ref[0, t, pl.ds(k, 1), :] - sk * b_ref[0, t, pl.ds(k, 1), :]
                 + vt * kd_ref[0, t, pl.ds(k, 1), :])
            s_ref[k] = s
            y = y + s * r_ref[t, pl.ds(k, 1), :]
        y_ref[0, t] = y
        return carry

    lax.fori_loop(0, steps, step, 0)


def _wkv_scan(r, kk, v, w, b, kd, t_ctx):
    t_all = r.shape[0]
    steps = min(SCAN_STEPS, t_ctx)
    nc = t_ctx // steps
    nblk = t_all // steps

    def blk(d, i):
        rev = jnp.where(i < nc, nc - 1 - i, nblk + nc - 1 - i)
        return jnp.where(d == 0, i, rev)

    shared = pl.BlockSpec((steps, RW_HEAD, LANES), lambda d, i: (blk(d, i), 0, 0))
    per_dir = pl.BlockSpec((1, steps, RW_HEAD, LANES), lambda d, i: (d, blk(d, i), 0, 0))
    return pl.pallas_call(
        functools.partial(_wkv_kernel, steps=steps),
        out_shape=jax.ShapeDtypeStruct((2, t_all, RW_HEAD, LANES), F32),
        grid=(2, nblk),
        in_specs=[shared, shared, shared, per_dir, per_dir, per_dir],
        out_specs=per_dir,
        scratch_shapes=[pltpu.VMEM((RW_HEAD, RW_HEAD, LANES), F32)],
        compiler_params=_cparams("arbitrary", "arbitrary"),
        name="wkv_scan",
    )(r, kk, v, w, b, kd)


def _rwpost_kernel(y_ref, bon_ref, g_ref, gng_ref, gnb_ref, hs_ref, o_ref):
    y = y_ref[...]
    inv = 1.0 / RW_HEAD
    yc = y - _head_sum(y, hs_ref) * inv
    var = _head_sum(yc * yc, hs_ref) * inv
    yn = yc * lax.rsqrt(var + RW_GN_EPS) * gng_ref[...] + gnb_ref[...]
    o_ref[...] = ((yn + bon_ref[...].astype(F32)) * g_ref[...].astype(F32)).astype(BF16)


def _rwkv_post(y, bon, g, prm):
    n, w = y.shape
    tb = 256
    row = pl.BlockSpec((tb, w), lambda i: (i, 0))
    vec = pl.BlockSpec((1, w), lambda i: (0, 0))
    return pl.pallas_call(
        _rwpost_kernel,
        out_shape=jax.ShapeDtypeStruct((n, w), BF16),
        grid=(n // tb,),
        in_specs=[row, row, row, vec, vec, pl.BlockSpec((w, w), lambda i: (0, 0))],
        out_specs=row,
        compiler_params=_cparams("parallel"),
        name="rwkv_post",
    )(y, bon, g, prm["gn_g"], prm["gn_b"], prm["hs"])


def _conv_kernel(p_ref, pp_ref, pn_ref, w_ref, b_ref, g_ref, beta_ref, o_ref, z_ref, *, tb, first_last):
    first, last = first_last(pl.program_id(0))
    c = CONV_WIDTH

    def glu(x):
        x = x.astype(F32)
        return x[:, :c] * _sigmoid(x[:, c:])

    z_ref[HALO:HALO + tb, :] = glu(p_ref[...])
    z_ref[0:HALO, :] = jnp.where(first, 0.0, glu(pp_ref[...]))
    z_ref[HALO + tb:2 * HALO + tb, :] = jnp.where(last, 0.0, glu(pn_ref[...]))
    acc = jnp.zeros((tb, c), F32)
    base = HALO - CONV_K // 2
    for j in range(CONV_K):
        acc = acc + z_ref[base + j:base + j + tb, :] * w_ref[j:j + 1, :]
    zn = _layer_norm(acc + b_ref[...], g_ref[...], beta_ref[...])
    o_ref[...] = (zn * _sigmoid(zn)).astype(BF16)


def _conformer_conv(p, prm, dims):
    n = p.shape[0]
    tb = min(ROW_TILE, dims[2])
    first_last, prev_blk, next_blk = _tile_flags(dims, tb)
    c = CONV_WIDTH
    cb = CONV_COL_BLOCK
    vec = pl.BlockSpec((1, c), lambda i: (0, 0))
    return pl.pallas_call(
        functools.partial(_conv_kernel, tb=tb, first_last=first_last),
        out_shape=jax.ShapeDtypeStruct((n, c), BF16),
        grid=(n // tb,),
        in_specs=[
            pl.BlockSpec((tb, 2 * c), lambda i: (i, cb)),
            pl.BlockSpec((HALO, 2 * c), lambda i: (prev_blk(i), cb)),
            pl.BlockSpec((HALO, 2 * c), lambda i: (next_blk(i), cb)),
            pl.BlockSpec((32, c), lambda i: (0, 0)), vec, vec, vec,
        ],
        out_specs=pl.BlockSpec((tb, c), lambda i: (i, 0)),
        scratch_shapes=[pltpu.VMEM((tb + 2 * HALO, c), F32)],
        compiler_params=_cparams("parallel"),
        name="conformer_conv",
    )(p, p, p, prm["cv_w"], prm["cv_b"], prm["cv_g"], prm["cv_beta"])


def _qkv_kernel(p_ref, cos_ref, sin_ref, qn_ref, kn_ref, q_o, k_o, v_o):
    lane = lax.broadcasted_iota(jnp.int32, (1, ATT_HEAD), 1)
    first_half = (lane % 64) < 32
    cosv = cos_ref[...]
    sinv = sin_ref[...]

    def norm_rope(x, gain, scale):
        x = x.astype(F32)
        xn = x * lax.rsqrt(jnp.mean(x * x, axis=-1, keepdims=True) + QK_EPS) * gain
        swapped = jnp.where(first_half, pltpu.roll(xn, ATT_HEAD - 32, axis=1), pltpu.roll(xn, 32, axis=1))
        return ((xn * cosv + swapped * sinv) * scale).astype(BF16)

    hd = ATT_HEAD
    for h in range(ATT_HEADS):
        q_o[h] = norm_rope(p_ref[:, h * hd:(h + 1) * hd], qn_ref[...], 1.0 / math.sqrt(ATT_HEAD))
    for h in range(ATT_KV_HEADS):
        k_o[h] = norm_rope(p_ref[:, ATT_Q + h * hd:ATT_Q + (h + 1) * hd], kn_ref[...], 1.0)
        v_o[h] = p_ref[:, ATT_Q + ATT_KV + h * hd:ATT_Q + ATT_KV + (h + 1) * hd]


def _qkv_post(p, cos_t, sin_t, qn, kn, dims):
    b, s, ctx = dims
    n = p.shape[0]
    tb = min(ROW_TILE, ctx)
    nlt = (b * s) // tb
    nbl = s // tb
    hd = ATT_HEAD

    def tab(i):
        return jnp.where(i < nlt, i % nbl, nbl)

    return pl.pallas_call(
        _qkv_kernel,
        out_shape=(jax.ShapeDtypeStruct((ATT_HEADS, n, hd), BF16),
                   jax.ShapeDtypeStruct((ATT_KV_HEADS, n, hd), BF16),
                   jax.ShapeDtypeStruct((ATT_KV_HEADS, n, hd), BF16)),
        grid=(n // tb,),
        in_specs=[
            pl.BlockSpec((tb, p.shape[1]), lambda i: (i, 0)),
            pl.BlockSpec((tb, hd), lambda i: (tab(i), 0)),
            pl.BlockSpec((tb, hd), lambda i: (tab(i), 0)),
            pl.BlockSpec((1, hd), lambda i: (0, 0)),
            pl.BlockSpec((1, hd), lambda i: (0, 0)),
        ],
        out_specs=(pl.BlockSpec((ATT_HEADS, tb, hd), lambda i: (0, i, 0)),
                   pl.BlockSpec((ATT_KV_HEADS, tb, hd), lambda i: (0, i, 0)),
                   pl.BlockSpec((ATT_KV_HEADS, tb, hd), lambda i: (0, i, 0))),
        compiler_params=_cparams("parallel"),
        name="qkv_norm_rope",
    )(p, cos_t, sin_t, qn.reshape(1, hd), kn.reshape(1, hd))


def _rope_tables(s, pad_rows):
    rows_n = s // GRID_W
    row = jnp.repeat(jnp.arange(rows_n), GRID_W)
    col = jnp.tile(jnp.arange(GRID_W), rows_n)
    inv = ROPE_THETA ** (-jnp.arange(ROPE_PAIRS, dtype=F32) / ROPE_PAIRS)
    ang_r = row[:, None] * inv
    ang_c = col[:, None] * inv
    cos_t = jnp.concatenate([jnp.cos(ang_r), jnp.cos(ang_r), jnp.cos(ang_c), jnp.cos(ang_c)], axis=1)
    sin_t = jnp.concatenate([-jnp.sin(ang_r), jnp.sin(ang_r), -jnp.sin(ang_c), jnp.sin(ang_c)], axis=1)
    cos_t = jnp.concatenate([cos_t, jnp.ones((pad_rows, ATT_HEAD), F32)], axis=0)
    sin_t = jnp.concatenate([sin_t, jnp.zeros((pad_rows, ATT_HEAD), F32)], axis=0)
    return cos_t, sin_t


def _attn_kernel(q_ref, *refs, seg_lens, kc):
    n_seg = len(seg_lens)
    o_ref, m_ref, l_ref, acc_ref = refs[2 * n_seg:]
    g, tq, hd = q_ref.shape
    q = q_ref[...].reshape(g * tq, hd)
    m_ref[...] = jnp.full(m_ref.shape, NEG_BIG, F32)
    l_ref[...] = jnp.zeros_like(l_ref)
    acc_ref[...] = jnp.zeros_like(acc_ref)

    def chunk(k, v):
        s = lax.dot_general(q, k, (((1,), (1,)), ((), ())), preferred_element_type=F32)
        m_old = m_ref[...]
        m_new = jnp.maximum(m_old, jnp.max(s, axis=-1, keepdims=True))
        p = jnp.exp(s - m_new)
        alpha = jnp.exp(m_old - m_new)
        l_ref[...] = alpha * l_ref[...] + jnp.sum(p, axis=-1, keepdims=True)
        acc_ref[...] = alpha * acc_ref[...] + jnp.dot(p.astype(BF16), v, preferred_element_type=F32)
        m_ref[...] = m_new

    for si, n_keys in enumerate(seg_lens):
        k_ref, v_ref = refs[2 * si], refs[2 * si + 1]
        c = min(kc, n_keys)
        if n_keys == c:
            chunk(k_ref[0], v_ref[0])
        else:
            def body(j, carry, k_ref=k_ref, v_ref=v_ref, c=c):
                off = pl.multiple_of(j * c, c)
                chunk(k_ref[0, pl.ds(off, c), :], v_ref[0, pl.ds(off, c), :])
                return carry

            lax.fori_loop(0, n_keys // c, body, 0)

    o = acc_ref[...] / l_ref[...]
    for gi in range(g):
        o_ref[:, gi * hd:(gi + 1) * hd] = o[gi * tq:(gi + 1) * tq, :].astype(o_ref.dtype)


def _attention(q, k, v, dims, ctx_queries):
    b, s, ctx = dims
    hd = ATT_HEAD
    g = ATT_GROUP
    nl = b * s
    ctx_blk = lambda kv, bi, qi: (kv, nl // ctx + bi, 0)
    if ctx_queries:
        tq, nq, rows = ctx, 1, b * ctx
        q_map = lambda bi, kv, qi: (kv, nl // ctx + bi, 0)
        segs = [(ctx, lambda bi, kv, qi: ctx_blk(kv, bi, qi))]
    else:
        tq = min(256, s)
        nq, rows = s // tq, nl
        q_map = lambda bi, kv, qi: (kv, bi * nq + qi, 0)
        segs = [(s, lambda bi, kv, qi: (kv, bi, 0)), (ctx, lambda bi, kv, qi: ctx_blk(kv, bi, qi))]
    in_specs = [pl.BlockSpec((g, tq, hd), q_map)]
    args = [q]
    for n_keys, kmap in segs:
        in_specs += [pl.BlockSpec((1, n_keys, hd), kmap), pl.BlockSpec((1, n_keys, hd), kmap)]
        args += [k, v]
    return pl.pallas_call(
        functools.partial(_attn_kernel, seg_lens=tuple(n for n, _ in segs), kc=512),
        out_shape=jax.ShapeDtypeStruct((rows, ATT_Q), BF16),
        grid=(b, ATT_KV_HEADS, nq),
        in_specs=in_specs,
        out_specs=pl.BlockSpec((tq, g * hd), lambda bi, kv, qi: (bi * nq + qi, kv)),
        scratch_shapes=[pltpu.VMEM((g * tq, 1), F32), pltpu.VMEM((g * tq, 1), F32), pltpu.VMEM((g * tq, hd), F32)],
        compiler_params=_cparams("parallel", "parallel", "arbitrary"),
        name="gqa_attention_ctx" if ctx_queries else "gqa_attention",
    )(*args)


def _router_kernel(x_ref, w_ref, bias_ref, tri_ref, idx_o, gate_o, rank_o, cnt_o, carry_ref):
    @pl.when(pl.program_id(0) == 0)
    def _():
        carry_ref[...] = jnp.zeros_like(carry_ref)

    tm = x_ref.shape[0]
    x = _unpack_pair(x_ref[...])
    scores = _sigmoid(jnp.dot(x, w_ref[...], preferred_element_type=F32))
    lane = lax.broadcasted_iota(jnp.int32, (tm, LANES), 1)
    sel = jnp.where(lane < N_EXPERTS, scores + bias_ref[...], NEG_BIG)
    picks = []
    gates = []
    chosen = jnp.zeros((tm, LANES), F32)
    for _ in range(TOP_K):
        mx = jnp.max(sel, axis=-1, keepdims=True)
        idx = jnp.min(jnp.where(sel == mx, lane, LANES), axis=-1, keepdims=True)
        hit = lane == idx
        picks.append((idx, hit))
        gates.append(jnp.sum(jnp.where(hit, scores, 0.0), axis=-1, keepdims=True))
        chosen = chosen + hit.astype(F32)
        sel = jnp.where(hit, NEG_BIG, sel)
    gsum = gates[0]
    for gj in gates[1:]:
        gsum = gsum + gj
    before = jnp.dot(tri_ref[...], chosen.astype(BF16), preferred_element_type=F32) + carry_ref[...]
    idx_out = jnp.zeros((tm, LANES), jnp.int32)
    gate_out = jnp.zeros((tm, LANES), F32)
    rank_out = jnp.zeros((tm, LANES), jnp.int32)
    for j, (idx, hit) in enumerate(picks):
        rank = jnp.sum(jnp.where(hit, before, 0.0), axis=-1, keepdims=True).astype(jnp.int32)
        idx_out = jnp.where(lane == j, idx, idx_out)
        gate_out = jnp.where(lane == j, ROUTE_SCALE * gates[j] / gsum, gate_out)
        rank_out = jnp.where(lane == j, rank, rank_out)
    idx_o[...] = idx_out
    gate_o[...] = gate_out
    rank_o[...] = rank_out
    carry_ref[...] = carry_ref[...] + jnp.sum(chosen, axis=0, keepdims=True)
    cnt_o[...] = carry_ref[...].astype(jnp.int32)


def _router(vfp, router_w, bias):
    n = vfp.shape[0]
    tm = ROUTER_TILE if n % ROUTER_TILE == 0 else 128
    d = router_w.shape[0]
    wpad = jnp.zeros((d, LANES), BF16).at[:, :N_EXPERTS].set(router_w.astype(BF16))
    bpad = jnp.zeros((1, LANES), F32).at[0, :N_EXPERTS].set(bias)
    tri = (lax.broadcasted_iota(jnp.int32, (tm, tm), 1) < lax.broadcasted_iota(jnp.int32, (tm, tm), 0)).astype(BF16)
    row = pl.BlockSpec((tm, LANES), lambda i: (i, 0))
    return pl.pallas_call(
        _router_kernel,
        out_shape=(jax.ShapeDtypeStruct((n, LANES), jnp.int32), jax.ShapeDtypeStruct((n, LANES), F32),
                   jax.ShapeDtypeStruct((n, LANES), jnp.int32), jax.ShapeDtypeStruct((1, LANES), jnp.int32)),
        grid=(n // tm,),
        in_specs=[
            pl.BlockSpec((tm, d // 2), lambda i: (i, 0)),
            pl.BlockSpec((d, LANES), lambda i: (0, 0)),
            pl.BlockSpec((1, LANES), lambda i: (0, 0)),
            pl.BlockSpec((tm, tm), lambda i: (0, 0)),
        ],
        out_specs=(row, row, row, pl.BlockSpec((1, LANES), lambda i: (0, 0))),
        scratch_shapes=[pltpu.VMEM((1, LANES), F32)],
        compiler_params=_cparams("arbitrary"),
        name="moe_router",
    )(vfp, wpad, bpad, tri)


def _dispatch_kernel(dest_ref, x_ref, xs_in_ref, xs_ref, sem):
    del xs_in_ref
    tm = x_ref.shape[0]

    def copy(r, d):
        return pltpu.make_async_copy(x_ref.at[pl.ds(r, 1)], xs_ref.at[pl.ds(d, 1)], sem)

    def start(r, carry):
        for j in range(TOP_K):
            copy(r, dest_ref[0, 0, r * TOP_K + j]).start()
        return carry

    lax.fori_loop(0, tm, start, 0)

    def wait(r, carry):
        for _ in range(TOP_K):
            copy(0, 0).wait()
        return carry

    lax.fori_loop(0, tm, wait, 0)


def _dispatch(vfp, dest, n_rows):
    n, kw = vfp.shape
    tm = DISPATCH_TILE if n % DISPATCH_TILE == 0 else 128
    dest3 = dest.reshape(n // tm, 1, tm * TOP_K)
    xs0 = jnp.zeros((n_rows, kw), U32)
    return pl.pallas_call(
        _dispatch_kernel,
        out_shape=jax.ShapeDtypeStruct((n_rows, kw), U32),
        grid=(n // tm,),
        in_specs=[
            pl.BlockSpec((1, 1, tm * TOP_K), lambda i: (i, 0, 0), memory_space=pltpu.SMEM),
            pl.BlockSpec((tm, kw), lambda i: (i, 0)),
            pl.BlockSpec(memory_space=pl.ANY),
        ],
        out_specs=pl.BlockSpec(memory_space=pl.ANY),
        scratch_shapes=[pltpu.SemaphoreType.DMA(())],
        input_output_aliases={2: 0},
        compiler_params=_cparams("arbitrary"),
        name="moe_dispatch",
    )(dest3, vfp, xs0)


def _ffn_kernel(be_ref, on_ref, x_ref, w1_ref, w3_ref, w2_ref, o_ref):
    del be_ref
    i = pl.program_id(0)

    @pl.when(on_ref[i] != 0)
    def _():
        x = _unpack_pair(x_ref[...])
        a = jnp.dot(x, w1_ref[0], preferred_element_type=F32)
        c = jnp.dot(x, w3_ref[0], preferred_element_type=F32)
        hmid = (a * _sigmoid(a) * c).astype(BF16)
        o_ref[...] = _pack_pair(jnp.dot(hmid, w2_ref[0], preferred_element_type=F32))

    @pl.when(on_ref[i] == 0)
    def _():
        o_ref[...] = jnp.zeros_like(o_ref)


def _grouped_ffn(xs, blk_e, blk_on, w1, w3, w2, tm):
    n_rows, kw = xs.shape
    d, ff = w1.shape[1], w1.shape[2]
    return pl.pallas_call(
        _ffn_kernel,
        out_shape=jax.ShapeDtypeStruct((n_rows, kw), U32),
        grid_spec=pltpu.PrefetchScalarGridSpec(
            num_scalar_prefetch=2,
            grid=(n_rows // tm,),
            in_specs=[
                pl.BlockSpec((tm, kw), lambda i, be, on: (i, 0)),
                pl.BlockSpec((1, d, ff), lambda i, be, on: (be[i], 0, 0)),
                pl.BlockSpec((1, d, ff), lambda i, be, on: (be[i], 0, 0)),
                pl.BlockSpec((1, ff, d), lambda i, be, on: (be[i], 0, 0)),
            ],
            out_specs=pl.BlockSpec((tm, kw), lambda i, be, on: (i, 0)),
        ),
        compiler_params=_cparams("arbitrary"),
        name="moe_ffn",
    )(blk_e, blk_on, xs, w1, w3, w2)


def _combine_kernel(dest_ref, gate_ref, sh_ref, h_ref, gf_ref, g_ref, b_ref, ys_ref, o_ref, buf_ref, sem):
    tm = h_ref.shape[0]

    def copy(r, j, d):
        return pltpu.make_async_copy(ys_ref.at[pl.ds(d, 1)], buf_ref.at[j, pl.ds(r, 1)], sem)

    def start(r, carry):
        for j in range(TOP_K):
            copy(r, j, dest_ref[0, 0, r * TOP_K + j]).start()
        return carry

    lax.fori_loop(0, tm, start, 0)

    def wait(r, carry):
        for _ in range(TOP_K):
            copy(0, 0, 0).wait()
        return carry

    lax.fori_loop(0, tm, wait, 0)

    f = _unpack_pair(sh_ref[...]).astype(F32)
    gate = gate_ref[...]
    for j in range(TOP_K):
        f = f + gate[:, j:j + 1] * _unpack_pair(buf_ref[j]).astype(F32)
    o_ref[...] = _layer_norm(DEEPNORM_ALPHA * h_ref[...] + gf_ref[0] * f, g_ref[...], b_ref[...])


def _combine(ys, dest, gate, shared, h, mod3, ln_g, ln_b, dims):
    n, d = h.shape
    kw = d // 2
    tm = COMBINE_TILE
    seq = _seq_of_tile(dims, tm)
    dest3 = dest.reshape(n // tm, 1, tm * TOP_K)
    return pl.pallas_call(
        _combine_kernel,
        out_shape=jax.ShapeDtypeStruct((n, d), F32),
        grid=(n // tm,),
        in_specs=[
            pl.BlockSpec((1, 1, tm * TOP_K), lambda i: (i, 0, 0), memory_space=pltpu.SMEM),
            pl.BlockSpec((tm, LANES), lambda i: (i, 0)),
            pl.BlockSpec((tm, kw), lambda i: (i, 0)),
            pl.BlockSpec((tm, d), lambda i: (i, 0)),
            pl.BlockSpec((1, 1, d), lambda i: (seq(i) * 6 + 5, 0, 0)),
            pl.BlockSpec((1, d), lambda i: (0, 0)),
            pl.BlockSpec((1, d), lambda i: (0, 0)),
            pl.BlockSpec(memory_space=pl.ANY),
        ],
        out_specs=pl.BlockSpec((tm, d), lambda i: (i, 0)),
        scratch_shapes=[pltpu.VMEM((TOP_K, tm, kw), U32), pltpu.SemaphoreType.DMA(())],
        compiler_params=_cparams("arbitrary"),
        name="moe_combine_ln",
    )(dest3, gate, shared, h, mod3, ln_g.reshape(1, d), ln_b.reshape(1, d), ys)


def _moe(vfp, h, mod3, router_w, bias, w1, w3, w2, sw1, sw3, sw2, ln_g, ln_b, dims):
    n = vfp.shape[0]
    tmb = MOE_BLOCK
    idx, gate, rank, cnt = _router(vfp, router_w, bias)
    counts = cnt[0, :N_EXPERTS]
    padded = (counts + tmb - 1) // tmb * tmb
    pend = jnp.cumsum(padded)
    pstart = pend - padded
    n_blocks = (n * TOP_K + N_EXPERTS * (tmb - 1) + tmb - 1) // tmb
    blk_pos = jnp.arange(n_blocks, dtype=jnp.int32) * tmb
    blk_e = jnp.minimum(jnp.sum((pend[None, :] <= blk_pos[:, None]).astype(jnp.int32), axis=1), N_EXPERTS - 1)
    blk_on = (blk_pos < pend[-1]).astype(jnp.int32)
    e_sel = idx[:, :TOP_K]
    onehot = e_sel[:, :, None] == jnp.arange(N_EXPERTS, dtype=jnp.int32)[None, None, :]
    dest = rank[:, :TOP_K] + jnp.sum(jnp.where(onehot, pstart[None, None, :], 0), axis=-1)
    dest = dest.reshape(-1).astype(jnp.int32)
    xs = _dispatch(vfp, dest, n_blocks * tmb)
    ys = _grouped_ffn(xs, blk_e, blk_on, w1, w3, w2, tmb)
    ones = jnp.ones((n // tmb,), jnp.int32)
    shared = _grouped_ffn(vfp, jnp.zeros((n // tmb,), jnp.int32), ones, sw1[None], sw3[None], sw2[None], tmb)
    return _combine(ys, dest, gate, shared, h, mod3, ln_g, ln_b, dims)


def _to_scan(x, dims, dtype=F32):
    b, s, ctx = dims
    nl = b * s

    def part(rows, t):
        return rows.reshape(b, t, RW_HEADS, RW_HEAD).transpose(1, 3, 0, 2).reshape(t, RW_HEAD, b * RW_HEADS)

    return jnp.concatenate([part(x[nl:], ctx), part(x[:nl], s)], axis=0).astype(dtype)


def _from_scan(y, dims):
    b, s, ctx = dims

    def part(rows, t):
        return rows.reshape(t, RW_HEAD, b, RW_HEADS).transpose(2, 0, 3, 1).reshape(b * t, RW_WIDTH)

    return jnp.concatenate([part(y[ctx:], s), part(y[:ctx], ctx)], axis=0)


def _even_params(j, even_w_in, rw_mu, rw_w0, rw_w_up, rw_a0, rw_a_up, rw_g_up, rw_kk, rw_ka, rw_rk, rw_gn_g,
                 rw_gn_b, cv_w, cv_b, cv_ln_g, cv_ln_b):
    w3 = 3 * RW_WIDTH
    o_g, o_w, o_a = w3, w3 + GATE_LORA, w3 + GATE_LORA + 2 * DECAY_LORA
    o_cv = o_a + 2 * ICLR_LORA

    def relay(m):
        lead = m.shape[0]
        z = lambda k: jnp.zeros((lead, k), m.dtype)
        pieces = [m[:, :w3], m[:, o_g:o_g + GATE_LORA], z(LANES - GATE_LORA)]
        for d in range(2):
            pieces += [m[:, o_w + d * DECAY_LORA:o_w + (d + 1) * DECAY_LORA], z(LANES - DECAY_LORA)]
        for d in range(2):
            pieces += [m[:, o_a + d * ICLR_LORA:o_a + (d + 1) * ICLR_LORA], z(LANES - ICLR_LORA)]
        return jnp.concatenate(pieces, axis=1)

    w_in = even_w_in[j]
    w_pad = jnp.concatenate([relay(w_in), jnp.zeros((w_in.shape[0], 2 * 2048 - RW_STREAM_PAD), w_in.dtype),
                             w_in[:, o_cv:]], axis=1).astype(BF16)
    pad_rows = lambda m, k: jnp.concatenate([m, jnp.zeros((k - m.shape[0], m.shape[1]), m.dtype)], axis=0)
    head = jnp.arange(RW_WIDTH) // RW_HEAD
    vec = lambda m: m.reshape(1, -1)
    return {
        "w_in": w_pad,
        "mu": relay(rw_mu[j]),
        "w0": rw_w0[j], "a0": rw_a0[j],
        "w_up": jnp.stack([pad_rows(rw_w_up[j, d], LANES) for d in range(2)]).astype(BF16),
        "a_up": jnp.stack([pad_rows(rw_a_up[j, d], LANES) for d in range(2)]).astype(BF16),
        "g_up": pad_rows(rw_g_up[j], LANES).astype(BF16),
        "kk": vec(rw_kk[j]), "ka": vec(rw_ka[j]), "rk": vec(rw_rk[j]),
        "hs": (head[:, None] == head[None, :]).astype(BF16),
        "gn_g": vec(rw_gn_g[j]), "gn_b": vec(rw_gn_b[j]),
        "cv_w": pad_rows(cv_w[j], 32), "cv_b": vec(cv_b[j]), "cv_g": vec(cv_ln_g[j]), "cv_beta": vec(cv_ln_b[j]),
    }


def _even_mixer(h, mod3, prm, dims):
    p = _inproj(h, mod3, prm["w_in"], dims, 1, 0)
    r, v, kk, g, bon, dec, bvec, kd = _rwkv_features(p, prm, dims)
    ts = functools.partial(_to_scan, dims=dims)
    stack2 = lambda x: jnp.stack([ts(x[0]), ts(x[1])])
    y = _wkv_scan(ts(r), ts(kk), ts(v), stack2(dec), stack2(bvec), stack2(kd), dims[2])
    rw = _rwkv_post(_from_scan(y[0] + y[1], dims), bon, g, prm)
    cv = _conformer_conv(p, prm, dims)
    return [rw, cv]


def _attn_mixer(h, mod3, w_in, qn, kn, tables, dims, with_ctx):
    p = _inproj(h, mod3, w_in, dims, 1, 0)
    q, k, v = _qkv_post(p, tables[0], tables[1], qn, kn, dims)
    out = _attention(q, k, v, dims, ctx_queries=False)
    if with_ctx:
        out_c = _attention(q, k, v, dims, ctx_queries=True)
    else:
        out_c = jnp.zeros((dims[0] * dims[2], ATT_Q), BF16)
    return [jnp.concatenate([out, out_c], axis=0)]


def kernel(x, c, ctx, c_ctx, ada_w, ada_b, ln1_g, ln1_b, ln2_g, ln2_b, even_w_in, even_w_out, rw_mu, rw_w0, rw_w_up, rw_a0, rw_a_up, rw_g_up, rw_kk, rw_ka, rw_rk, rw_gn_g, rw_gn_b, cv_w, cv_b, cv_ln_g, cv_ln_b, odd_w_in, odd_w_out, q_norm, k_norm, moe_router, moe_bias, moe_w1, moe_w3, moe_w2, sh_w1, sh_w3, sh_w2):
    bsz, s_len, d = x.shape
    ctx_len = ctx.shape[1]
    dims = (bsz, s_len, ctx_len)
    assert bsz * RW_HEADS == LANES, "the WKV scan maps (batch, head) pairs onto the 128 lanes"
    n_lat = bsz * s_len
    depth = ada_w.shape[0]

    c16 = jnp.zeros((16, d), F32).at[:bsz].set(c).at[bsz].set(c_ctx)
    mod_all = _ada_all(c16, ada_w, ada_b)[:, :bsz + 1].reshape(depth, (bsz + 1) * 6, 1, d)
    tables = _rope_tables(s_len, min(ROW_TILE, ctx_len))

    h = jnp.concatenate([x.reshape(n_lat, d), ctx.reshape(bsz * ctx_len, d)], axis=0)
    for layer in range(depth):
        j = layer // 2
        last = layer == depth - 1
        mod3 = mod_all[layer]
        if layer % 2 == 0:
            prm = _even_params(j, even_w_in, rw_mu, rw_w0, rw_w_up, rw_a0, rw_a_up, rw_g_up, rw_kk, rw_ka, rw_rk,
                               rw_gn_g, rw_gn_b, cv_w, cv_b, cv_ln_g, cv_ln_b)
            acts = _even_mixer(h, mod3, prm, dims)
            w_out = even_w_out[j].astype(BF16)
        else:
            acts = _attn_mixer(h, mod3, odd_w_in[j].astype(BF16), q_norm[j], k_norm[j], tables, dims,
                               with_ctx=not last)
            w_out = odd_w_out[j].astype(BF16)
        h, vfp = _outproj(acts, w_out, h, mod3, ln1_g[layer], ln1_b[layer], dims)
        h = _moe(vfp, h, mod3, moe_router[layer], moe_bias[layer], moe_w1[layer].astype(BF16),
                 moe_w3[layer].astype(BF16), moe_w2[layer].astype(BF16), sh_w1[layer].astype(BF16),
                 sh_w3[layer].astype(BF16), sh_w2[layer].astype(BF16), ln2_g[layer], ln2_b[layer], dims)
    return h[:n_lat].reshape(bsz, s_len, d)
```

```python
import functools
import math

import jax
import jax.numpy as jnp
from jax import lax
from jax.experimental import pallas as pl
from jax.experimental.pallas import tpu as pltpu

F32 = jnp.float32
BF16 = jnp.bfloat16
U32 = jnp.uint32

D_MODEL = 2048
DEPTH = 4
GRID_W = 64

RW_HEADS = 16
RW_HEAD = 64
RW_WIDTH = RW_HEADS * RW_HEAD
DECAY_LORA = 96
ICLR_LORA = 96
GATE_LORA = 64
RW_GN_EPS = 64e-5
CONV_WIDTH = 1024
CONV_K = 31

ATT_HEADS = 16
ATT_KV_HEADS = 4
ATT_HEAD = 128
ATT_GROUP = ATT_HEADS // ATT_KV_HEADS
ATT_Q = ATT_HEADS * ATT_HEAD
ATT_KV = ATT_KV_HEADS * ATT_HEAD
ROPE_THETA = 10000.0
ROPE_PAIRS = ATT_HEAD // 4
QK_EPS = 1e-6

N_EXPERTS = 64
TOP_K = 6
EXPERT_FF = 384
ROUTE_SCALE = 2.5

DEEPNORM_ALPHA = (2 * DEPTH) ** 0.25
LN_EPS = 1e-5

LANES = 128
SMALL_W = 640
RW_STREAM_PAD = 3 * RW_WIDTH + SMALL_W
EVEN_IN_PAD = 6144
CONV_COL_BLOCK = 2
EXP_M05 = math.exp(-0.5)
NEG_BIG = -1e30
VMEM_LIMIT = 56 * 1024 * 1024

ROW_TILE = 256
HALO = 16
SCAN_STEPS = 32
MOE_BLOCK = 256
ROUTER_TILE = 512
DISPATCH_TILE = 256
COMBINE_TILE = 128
ATTN_ROW_BLOCK = 128


def _cparams(*sem):
    return pltpu.CompilerParams(dimension_semantics=tuple(sem), vmem_limit_bytes=VMEM_LIMIT)


def _layer_norm(x, g, b):
    mu = jnp.mean(x, axis=-1, keepdims=True)
    xc = x - mu
    var = jnp.mean(xc * xc, axis=-1, keepdims=True)
    return xc * lax.rsqrt(var + LN_EPS) * g + b


def _sigmoid(x):
    return 1.0 / (1.0 + jnp.exp(-x))


def _pack_pair(x):
    k = x.shape[1] // 2
    lo = lax.bitcast_convert_type(x[:, :k].astype(BF16).astype(F32), U32) >> 16
    hi = lax.bitcast_convert_type(x[:, k:].astype(BF16).astype(F32), U32) & jnp.uint32(0xFFFF0000)
    return hi | lo


def _unpack_pair(w):
    lo = lax.bitcast_convert_type(w << 16, F32)
    hi = lax.bitcast_convert_type(w & jnp.uint32(0xFFFF0000), F32)
    return jnp.concatenate([lo, hi], axis=1).astype(BF16)


def _col_tile(n, cap=1280):
    best = LANES
    for t in range(LANES, cap + 1, LANES):
        if n % t == 0:
            best = t
    return best


def _ada_kernel(c_ref, w_ref, b_ref, o_ref):
    c = c_ref[...]
    s = (c * _sigmoid(c)).astype(BF16)
    o_ref[0] = jnp.dot(s, w_ref[0].astype(BF16), preferred_element_type=F32) + b_ref[0]


def _ada_all(c16, ada_w, ada_b):
    depth, d, n6 = ada_w.shape
    tn = 1024
    return pl.pallas_call(
        _ada_kernel,
        out_shape=jax.ShapeDtypeStruct((depth, 16, n6), F32),
        grid=(depth, n6 // tn),
        in_specs=[
            pl.BlockSpec((16, d), lambda l, j: (0, 0)),
            pl.BlockSpec((1, d, tn), lambda l, j: (l, 0, j)),
            pl.BlockSpec((1, 1, tn), lambda l, j: (l, 0, j)),
        ],
        out_specs=pl.BlockSpec((1, 16, tn), lambda l, j: (l, 0, j)),
        compiler_params=_cparams("parallel", "parallel"),
        name="ada_mod",
    )(c16, ada_w, ada_b.reshape(depth, 1, n6))


def _inproj_kernel(x_ref, sc_ref, sh_ref, w_ref, o_ref, u_ref):
    @pl.when(pl.program_id(1) == 0)
    def _():
        u_ref[...] = (x_ref[...] * (1.0 + sc_ref[0]) + sh_ref[0]).astype(BF16)

    o_ref[...] = jnp.dot(u_ref[...], w_ref[...], preferred_element_type=F32).astype(o_ref.dtype)


def _inproj(h, mod3, w, dims, sc_idx, sh_idx):
    n, d = h.shape
    nout = w.shape[1]
    tm = min(512, dims[1])
    tn = _col_tile(nout)
    seq = _seq_of_tile(dims, tm)
    return pl.pallas_call(
        _inproj_kernel,
        out_shape=jax.ShapeDtypeStruct((n, nout), BF16),
        grid=(n // tm, nout // tn),
        in_specs=[
            pl.BlockSpec((tm, d), lambda i, j: (i, 0)),
            pl.BlockSpec((1, 1, d), lambda i, j: (seq(i) * 6 + sc_idx, 0, 0)),
            pl.BlockSpec((1, 1, d), lambda i, j: (seq(i) * 6 + sh_idx, 0, 0)),
            pl.BlockSpec((d, tn), lambda i, j: (0, j)),
        ],
        out_specs=pl.BlockSpec((tm, tn), lambda i, j: (i, j)),
        scratch_shapes=[pltpu.VMEM((tm, d), BF16)],
        compiler_params=_cparams("parallel", "arbitrary"),
        name="inproj",
    )(h, mod3, mod3, w)


def _seq_of_tile(dims, tm):
    b, s, _ = dims

    def seq(i):
        return jnp.minimum((i * tm) // s, b)

    return seq


def _outproj_kernel(*refs, splits):
    n_in = len(splits)
    a_refs = refs[:n_in]
    w_ref, h_ref, ga_ref, scf_ref, shf_ref, g_ref, b_ref, ho_ref, vf_ref = refs[n_in:]
    acc = None
    off = 0
    for a_ref, k in zip(a_refs, splits):
        part = jnp.dot(a_ref[...], w_ref[off:off + k, :], preferred_element_type=F32)
        acc = part if acc is None else acc + part
        off += k
    hn = _layer_norm(DEEPNORM_ALPHA * h_ref[...] + ga_ref[0] * acc, g_ref[...], b_ref[...])
    ho_ref[...] = hn
    vf_ref[...] = _pack_pair(hn * (1.0 + scf_ref[0]) + shf_ref[0])


def _outproj(acts, w, h, mod3, ln_g, ln_b, dims):
    n, d = h.shape
    tm = min(256, dims[1])
    seq = _seq_of_tile(dims, tm)
    splits = tuple(a.shape[1] for a in acts)
    in_specs = [pl.BlockSpec((tm, k), lambda i: (i, 0)) for k in splits]
    in_specs += [
        pl.BlockSpec(w.shape, lambda i: (0, 0)),
        pl.BlockSpec((tm, d), lambda i: (i, 0)),
        pl.BlockSpec((1, 1, d), lambda i: (seq(i) * 6 + 2, 0, 0)),
        pl.BlockSpec((1, 1, d), lambda i: (seq(i) * 6 + 4, 0, 0)),
        pl.BlockSpec((1, 1, d), lambda i: (seq(i) * 6 + 3, 0, 0)),
        pl.BlockSpec((1, d), lambda i: (0, 0)),
        pl.BlockSpec((1, d), lambda i: (0, 0)),
    ]
    return pl.pallas_call(
        functools.partial(_outproj_kernel, splits=splits),
        out_shape=(jax.ShapeDtypeStruct((n, d), F32), jax.ShapeDtypeStruct((n, d // 2), U32)),
        grid=(n // tm,),
        in_specs=in_specs,
        out_specs=(pl.BlockSpec((tm, d), lambda i: (i, 0)), pl.BlockSpec((tm, d // 2), lambda i: (i, 0))),
        compiler_params=_cparams("parallel"),
        name="outproj_ln",
    )(*acts, w, h, mod3, mod3, mod3, ln_g.reshape(1, d), ln_b.reshape(1, d))


def _tile_flags(dims, tb):
    b, s, ctx = dims
    n = b * (s + ctx)
    nlt = (b * s) // tb
    nbl = s // tb
    nbc = ctx // tb

    def first_last(i):
        il = i % nbl
        ic = (i - nlt) % nbc
        lat = i < nlt
        first = jnp.where(lat, il == 0, ic == 0)
        last = jnp.where(lat, il == nbl - 1, ic == nbc - 1)
        return first, last

    def prev_blk(i):
        return jnp.maximum((i * tb) // HALO - 1, 0)

    def next_blk(i):
        return jnp.minimum(((i + 1) * tb) // HALO, n // HALO - 1)

    return first_last, prev_blk, next_blk


def _head_sum(x, hs_ref):
    hi = x.astype(BF16)
    lo = (x - hi.astype(F32)).astype(BF16)
    hs = hs_ref[...]
    return jnp.dot(hi, hs, preferred_element_type=F32) + jnp.dot(lo, hs, preferred_element_type=F32)


def _feat_kernel(p_ref, pp_ref, pn_ref, mu_ref, w0_ref, a0_ref, wup_ref, aup_ref, gup_ref, kk_ref, ka_ref,
                 rk_ref, hs_ref, r_o, v_o, kkn_o, g_o, bon_o, dec_o, b_o, kd_o, buf_ref, *, tb, first_last):
    first, last = first_last(pl.program_id(0))
    w = RW_WIDTH
    buf_ref[8:8 + tb, :] = p_ref[...].astype(F32)
    buf_ref[7:8, :] = jnp.where(first, 0.0, pp_ref[HALO - 1:HALO, :].astype(F32))
    buf_ref[8 + tb:9 + tb, :] = jnp.where(last, 0.0, pn_ref[0:1, :].astype(F32))

    def shifted(c0, c1):
        cur = buf_ref[8:8 + tb, c0:c1]
        prev = buf_ref[7:7 + tb, c0:c1]
        nxt = buf_ref[9:9 + tb, c0:c1]
        return cur + mu_ref[0:1, c0:c1] * (prev - cur) + mu_ref[1:2, c0:c1] * (nxt - cur)

    r = shifted(0, w)
    k = shifted(w, 2 * w)
    v = shifted(2 * w, 3 * w)
    sm = shifted(3 * w, 3 * w + SMALL_W)

    g = jnp.dot(_sigmoid(sm[:, 0:LANES]).astype(BF16), gup_ref[...], preferred_element_type=F32)
    kkr = k * kk_ref[...]
    kk = kkr * lax.rsqrt(_head_sum(kkr * kkr, hs_ref) + 1e-12)

    ksum = None
    for d in range(2):
        wd = sm[:, LANES * (1 + d):LANES * (2 + d)]
        ad = sm[:, LANES * (3 + d):LANES * (4 + d)]
        xw = w0_ref[d:d + 1, :] + jnp.dot(jnp.tanh(wd).astype(BF16), wup_ref[d], preferred_element_type=F32)
        dec_o[d] = jnp.exp(-EXP_M05 * _sigmoid(xw))
        a = _sigmoid(a0_ref[d:d + 1, :] + jnp.dot(ad.astype(BF16), aup_ref[d], preferred_element_type=F32))
        b_o[d] = (a * kk).astype(BF16)
        kd = k * (1.0 + (a - 1.0) * ka_ref[...])
        kd_o[d] = kd.astype(BF16)
        ksum = kd if ksum is None else ksum + kd

    bon_o[...] = (_head_sum(r * ksum * rk_ref[...], hs_ref) * v).astype(BF16)
    r_o[...] = r.astype(BF16)
    v_o[...] = v.astype(BF16)
    kkn_o[...] = kk.astype(BF16)
    g_o[...] = g.astype(BF16)


def _rwkv_features(p, prm, dims):
    n = p.shape[0]
    tb = min(ROW_TILE, dims[2])
    first_last, prev_blk, next_blk = _tile_flags(dims, tb)
    w = RW_WIDTH
    pw = RW_STREAM_PAD
    full2 = lambda shape: pl.BlockSpec(shape, lambda i: (0, 0))
    full3 = lambda shape: pl.BlockSpec(shape, lambda i: (0, 0, 0))
    row = pl.BlockSpec((tb, w), lambda i: (i, 0))
    row2 = pl.BlockSpec((2, tb, w), lambda i: (0, i, 0))
    sd = lambda dt: jax.ShapeDtypeStruct((n, w), dt)
    sd2 = lambda dt: jax.ShapeDtypeStruct((2, n, w), dt)
    return pl.pallas_call(
        functools.partial(_feat_kernel, tb=tb, first_last=first_last),
        out_shape=(sd(BF16), sd(BF16), sd(BF16), sd(BF16), sd(BF16), sd2(F32), sd2(BF16), sd2(BF16)),
        grid=(n // tb,),
        in_specs=[
            pl.BlockSpec((tb, pw), lambda i: (i, 0)),
            pl.BlockSpec((HALO, pw), lambda i: (prev_blk(i), 0)),
            pl.BlockSpec((HALO, pw), lambda i: (next_blk(i), 0)),
            full2((2, pw)), full2((2, w)), full2((2, w)),
            full3((2, LANES, w)), full3((2, LANES, w)), full2((LANES, w)),
            full2((1, w)), full2((1, w)), full2((1, w)), full2((w, w)),
        ],
        out_specs=(row, row, row, row, row, row2, row2, row2),
        scratch_shapes=[pltpu.VMEM((tb + 16, pw), F32)],
        compiler_params=_cparams("parallel"),
        name="rwkv_features",
    )(p, p, p, prm["mu"], prm["w0"], prm["a0"], prm["w_up"], prm["a_up"], prm["g_up"], prm["kk"], prm["ka"],
      prm["rk"], prm["hs"])


def _wkv_kernel(r_ref, kk_ref, v_ref, w_ref, b_ref, kd_ref, s0_ref, y_ref, st_ref, s_ref, rf, kkf, vf, bf, kdf, *,
                steps, reverse):
    i = pl.program_id(0)

    @pl.when(i == 0)
    def _():
        s_ref[...] = s0_ref[...]

    for src, dst in ((r_ref, rf), (kk_ref, kkf), (v_ref, vf), (b_ref, bf), (kd_ref, kdf)):
        dst[...] = src[...].astype(F32)

    nk = RW_HEAD
    half = nk // 2
    t_first = steps - 1 if reverse else 0

    def s_dot_kk(t, lo):
        acc = jnp.zeros((half, LANES), F32)
        for k in range(nk):
            acc = acc + s_ref[k, lo:lo + half, :] * kkf[t, pl.ds(k, 1), :]
        return acc

    def step(j, sks):
        t = steps - 1 - j if reverse else j
        t_next = jnp.maximum(t - 1, 0) if reverse else jnp.minimum(t + 1, steps - 1)
        sks_next = []
        for sk, lo in zip(sks, (0, half)):
            vt = vf[t, lo:lo + half, :]
            y = jnp.zeros((half, LANES), F32)
            sk_next = jnp.zeros((half, LANES), F32)
            for k in range(nk):
                s = (s_ref[k, lo:lo + half, :] * w_ref[t, pl.ds(k, 1), :] - sk * bf[t, pl.ds(k, 1), :]
                     + vt * kdf[t, pl.ds(k, 1), :])
                s_ref[k, lo:lo + half, :] = s
                y = y + s * rf[t, pl.ds(k, 1), :]
                sk_next = sk_next + s * kkf[t_next, pl.ds(k, 1), :]
            y_ref[t, lo:lo + half, :] = y
            sks_next.append(sk_next)
        return tuple(sks_next)

    lax.fori_loop(0, steps, step, (s_dot_kk(t_first, 0), s_dot_kk(t_first, half)))

    @pl.when(i == pl.num_programs(0) - 1)
    def _():
        st_ref[...] = s_ref[...]


def _wkv_scan(r, kk, v, w, b, kd, s0, reverse):
    t_all = r.shape[0]
    steps = min(SCAN_STEPS, t_all)
    nblk = t_all // steps
    blk = (lambda i: (nblk - 1 - i, 0, 0)) if reverse else (lambda i: (i, 0, 0))
    seq = pl.BlockSpec((steps, RW_HEAD, LANES), blk)
    state = pl.BlockSpec((RW_HEAD, RW_HEAD, LANES), lambda i: (0, 0, 0))
    buf = pltpu.VMEM((steps, RW_HEAD, LANES), F32)
    return pl.pallas_call(
        functools.partial(_wkv_kernel, steps=steps, reverse=reverse),
        out_shape=(jax.ShapeDtypeStruct((t_all, RW_HEAD, LANES), F32),
                   jax.ShapeDtypeStruct((RW_HEAD, RW_HEAD, LANES), F32)),
        grid=(nblk,),
        in_specs=[seq, seq, seq, seq, seq, seq, state],
        out_specs=(seq, state),
        scratch_shapes=[pltpu.VMEM((RW_HEAD, RW_HEAD, LANES), F32), buf, buf, buf, buf, buf],
        compiler_params=_cparams("arbitrary"),
        name="wkv_scan_bwd" if reverse else "wkv_scan_fwd",
    )(r, kk, v, w, b, kd, s0)


def _rwpost_kernel(y_ref, bon_ref, g_ref, gng_ref, gnb_ref, hs_ref, o_ref):
    y = y_ref[...]
    inv = 1.0 / RW_HEAD
    yc = y - _head_sum(y, hs_ref) * inv
    var = _head_sum(yc * yc, hs_ref) * inv
    yn = yc * lax.rsqrt(var + RW_GN_EPS) * gng_ref[...] + gnb_ref[...]
    o_ref[...] = ((yn + bon_ref[...].astype(F32)) * g_ref[...].astype(F32)).astype(BF16)


def _rwkv_post(y, bon, g, prm):
    n, w = y.shape
    tb = 256
    row = pl.BlockSpec((tb, w), lambda i: (i, 0))
    vec = pl.BlockSpec((1, w), lambda i: (0, 0))
    return pl.pallas_call(
        _rwpost_kernel,
        out_shape=jax.ShapeDtypeStruct((n, w), BF16),
        grid=(n // tb,),
        in_specs=[row, row, row, vec, vec, pl.BlockSpec((w, w), lambda i: (0, 0))],
        out_specs=row,
        compiler_params=_cparams("parallel"),
        name="rwkv_post",
    )(y, bon, g, prm["gn_g"], prm["gn_b"], prm["hs"])


def _conv_kernel(p_ref, pp_ref, pn_ref, w_ref, b_ref, g_ref, beta_ref, o_ref, z_ref, *, tb, first_last):
    first, last = first_last(pl.program_id(0))
    c = CONV_WIDTH

    def glu(x):
        x = x.astype(F32)
        return x[:, :c] * _sigmoid(x[:, c:])

    z_ref[HALO:HALO + tb, :] = glu(p_ref[...])
    z_ref[0:HALO, :] = jnp.where(first, 0.0, glu(pp_ref[...]))
    z_ref[HALO + tb:2 * HALO + tb, :] = jnp.where(last, 0.0, glu(pn_ref[...]))
    acc = jnp.zeros((tb, c), F32)
    base = HALO - CONV_K // 2
    for j in range(CONV_K):
        acc = acc + z_ref[base + j:base + j + tb, :] * w_ref[j:j + 1, :]
    zn = _layer_norm(acc + b_ref[...], g_ref[...], beta_ref[...])
    o_ref[...] = (zn * _sigmoid(zn)).astype(BF16)


def _conformer_conv(p, prm, dims):
    n = p.shape[0]
    tb = min(ROW_TILE, dims[2])
    first_last, prev_blk, next_blk = _tile_flags(dims, tb)
    c = CONV_WIDTH
    cb = CONV_COL_BLOCK
    vec = pl.BlockSpec((1, c), lambda i: (0, 0))
    return pl.pallas_call(
        functools.partial(_conv_kernel, tb=tb, first_last=first_last),
        out_shape=jax.ShapeDtypeStruct((n, c), BF16),
        grid=(n // tb,),
        in_specs=[
            pl.BlockSpec((tb, 2 * c), lambda i: (i, cb)),
            pl.BlockSpec((HALO, 2 * c), lambda i: (prev_blk(i), cb)),
            pl.BlockSpec((HALO, 2 * c), lambda i: (next_blk(i), cb)),
            pl.BlockSpec((32, c), lambda i: (0, 0)), vec, vec, vec,
        ],
        out_specs=pl.BlockSpec((tb, c), lambda i: (i, 0)),
        scratch_shapes=[pltpu.VMEM((tb + 2 * HALO, c), F32)],
        compiler_params=_cparams("parallel"),
        name="conformer_conv",
    )(p, p, p, prm["cv_w"], prm["cv_b"], prm["cv_g"], prm["cv_beta"])


def _qkv_kernel(p_ref, cos_ref, sin_ref, qn_ref, kn_ref, q_o, k_o, v_o):
    lane = lax.broadcasted_iota(jnp.int32, (1, ATT_HEAD), 1)
    first_half = (lane % 64) < 32
    cosv = cos_ref[...]
    sinv = sin_ref[...]

    def norm_rope(x, gain, scale):
        x = x.astype(F32)
        xn = x * lax.rsqrt(jnp.mean(x * x, axis=-1, keepdims=True) + QK_EPS) * gain
        swapped = jnp.where(first_half, pltpu.roll(xn, ATT_HEAD - 32, axis=1), pltpu.roll(xn, 32, axis=1))
        return (xn * cosv + swapped * sinv) * scale

    hd = ATT_HEAD
    ones = jnp.ones((p_ref.shape[0], hd), BF16)
    for h in range(ATT_HEADS):
        q_o[h] = norm_rope(p_ref[:, h * hd:(h + 1) * hd], qn_ref[...], 1.0 / math.sqrt(ATT_HEAD)).astype(BF16)
    for h in range(ATT_KV_HEADS):
        k_o[h] = norm_rope(p_ref[:, ATT_Q + h * hd:ATT_Q + (h + 1) * hd], kn_ref[...], 1.0).T.astype(BF16)
        v_o[h, :, :hd] = p_ref[:, ATT_Q + ATT_KV + h * hd:ATT_Q + ATT_KV + (h + 1) * hd]
        v_o[h, :, hd:] = ones


def _qkv_post(p, cos_t, sin_t, qn, kn, dims):
    b, s, ctx = dims
    n = p.shape[0]
    tb = min(ROW_TILE, ctx)
    nlt = (b * s) // tb
    nbl = s // tb
    hd = ATT_HEAD

    def tab(i):
        return jnp.where(i < nlt, i % nbl, nbl)

    return pl.pallas_call(
        _qkv_kernel,
        out_shape=(jax.ShapeDtypeStruct((ATT_HEADS, n, hd), BF16),
                   jax.ShapeDtypeStruct((ATT_KV_HEADS, hd, n), BF16),
                   jax.ShapeDtypeStruct((ATT_KV_HEADS, n, 2 * hd), BF16)),
        grid=(n // tb,),
        in_specs=[
            pl.BlockSpec((tb, p.shape[1]), lambda i: (i, 0)),
            pl.BlockSpec((tb, hd), lambda i: (tab(i), 0)),
            pl.BlockSpec((tb, hd), lambda i: (tab(i), 0)),
            pl.BlockSpec((1, hd), lambda i: (0, 0)),
            pl.BlockSpec((1, hd), lambda i: (0, 0)),
        ],
        out_specs=(pl.BlockSpec((ATT_HEADS, tb, hd), lambda i: (0, i, 0)),
                   pl.BlockSpec((ATT_KV_HEADS, hd, tb), lambda i: (0, 0, i)),
                   pl.BlockSpec((ATT_KV_HEADS, tb, 2 * hd), lambda i: (0, i, 0))),
        compiler_params=_cparams("parallel"),
        name="qkv_norm_rope",
    )(p, cos_t, sin_t, qn.reshape(1, hd), kn.reshape(1, hd))


def _rope_tables(s, pad_rows):
    rows_n = s // GRID_W
    row = jnp.repeat(jnp.arange(rows_n), GRID_W)
    col = jnp.tile(jnp.arange(GRID_W), rows_n)
    inv = ROPE_THETA ** (-jnp.arange(ROPE_PAIRS, dtype=F32) / ROPE_PAIRS)
    ang_r = row[:, None] * inv
    ang_c = col[:, None] * inv
    cos_t = jnp.concatenate([jnp.cos(ang_r), jnp.cos(ang_r), jnp.cos(ang_c), jnp.cos(ang_c)], axis=1)
    sin_t = jnp.concatenate([-jnp.sin(ang_r), jnp.sin(ang_r), -jnp.sin(ang_c), jnp.sin(ang_c)], axis=1)
    cos_t = jnp.concatenate([cos_t, jnp.ones((pad_rows, ATT_HEAD), F32)], axis=0)
    sin_t = jnp.concatenate([sin_t, jnp.zeros((pad_rows, ATT_HEAD), F32)], axis=0)
    return cos_t, sin_t


def _attn_kernel(q_ref, *refs, seg_lens, kc):
    n_seg = len(seg_lens)
    o_ref, s_ref, mx_ref, p_ref = refs[2 * n_seg:]
    g, tq, hd = q_ref.shape

    @pl.when(pl.program_id(0) == 0)
    def _():
        p_ref[...] = jnp.zeros_like(p_ref)

    acc = None
    off = 0
    for si, n_keys in enumerate(seg_lens):
        pv = jnp.dot(p_ref[:, off:off + n_keys], refs[2 * si + 1][0], preferred_element_type=F32)
        acc = pv if acc is None else acc + pv
        off += n_keys

    q = q_ref[...].reshape(g * tq, hd)
    chunks = []
    off = 0
    for si, n_keys in enumerate(seg_lens):
        for c0 in range(0, n_keys, kc):
            c = min(kc, n_keys - c0)
            chunks.append((si, c0, c, off))
            off += c
    for n, (si, c0, c, off) in enumerate(chunks):
        s = jnp.dot(q, refs[2 * si][0, :, c0:c0 + c], preferred_element_type=F32)
        s_ref[:, off:off + c] = s
        part = s[:, :LANES]
        for j in range(1, c // LANES):
            part = jnp.maximum(part, s[:, j * LANES:(j + 1) * LANES])
        mx_ref[...] = part if n == 0 else jnp.maximum(mx_ref[...], part)

    o = acc[:, :hd] / jnp.maximum(acc[:, hd:], 1e-30)
    for gi in range(g):
        o_ref[:, gi * hd:(gi + 1) * hd] = o[gi * tq:(gi + 1) * tq, :].astype(o_ref.dtype)

    mx_ref[...] = jnp.broadcast_to(jnp.max(mx_ref[...], axis=-1, keepdims=True), mx_ref.shape)
    rb = min(ATTN_ROW_BLOCK, g * tq)

    def exp_rows(i, carry):
        r0 = pl.multiple_of(i * rb, rb)
        mb = mx_ref[pl.ds(r0, rb), :]
        for _, _, c, off in chunks:
            e = jnp.exp(s_ref[pl.ds(r0, rb), off:off + c] - jnp.concatenate([mb] * (c // LANES), axis=1))
            p_ref[pl.ds(r0, rb), off:off + c] = e.astype(BF16)
        return carry

    lax.fori_loop(0, (g * tq) // rb, exp_rows, 0)


def _attention(q, kt, v1, dims, ctx_queries):
    b, s, ctx = dims
    hd = ATT_HEAD
    g = ATT_GROUP
    kvh = ATT_KV_HEADS
    nl = b * s
    if ctx_queries:
        tq, nq, rows, q_base = ctx, 1, b * ctx, nl // ctx
        segs = [(ctx, nl // ctx)]
    else:
        tq = min(256, s)
        nq, rows, q_base = s // tq, nl, 0
        segs = [(s, 0), (ctx, nl // ctx)]
    n_tiles = b * kvh * nq

    def split(tile):
        return tile // (kvh * nq), (tile // nq) % kvh, tile % nq

    cur = lambda t: split(jnp.minimum(t, n_tiles - 1))
    prev = lambda t: split(jnp.maximum(t - 1, 0))

    def q_map(t):
        bi, kv, qi = cur(t)
        return kv, q_base + bi * nq + qi, 0

    def o_map(t):
        bi, kv, qi = prev(t)
        return bi * nq + qi, kv

    in_specs = [pl.BlockSpec((g, tq, hd), q_map)]
    args = [q]
    for n_keys, base in segs:
        def k_map(t, base=base):
            bi, kv, _ = cur(t)
            return kv, 0, base + bi

        def v_map(t, base=base):
            bi, kv, _ = prev(t)
            return kv, base + bi, 0

        in_specs += [pl.BlockSpec((1, hd, n_keys), k_map), pl.BlockSpec((1, n_keys, 2 * hd), v_map)]
        args += [kt, v1]
    n_all = sum(n for n, _ in segs)
    return pl.pallas_call(
        functools.partial(_attn_kernel, seg_lens=tuple(n for n, _ in segs), kc=512),
        out_shape=jax.ShapeDtypeStruct((rows, ATT_Q), BF16),
        grid=(n_tiles + 1,),
        in_specs=in_specs,
        out_specs=pl.BlockSpec((tq, g * hd), o_map),
        scratch_shapes=[pltpu.VMEM((g * tq, n_all), F32), pltpu.VMEM((g * tq, LANES), F32),
                        pltpu.VMEM((g * tq, n_all), BF16)],
        compiler_params=_cparams("arbitrary"),
        name="gqa_attention_ctx" if ctx_queries else "gqa_attention",
    )(*args)


def _router_kernel(x_ref, w_ref, bias_ref, tri_ref, idx_o, gate_o, rank_o, cnt_o, carry_ref):
    @pl.when(pl.program_id(0) == 0)
    def _():
        carry_ref[...] = jnp.zeros_like(carry_ref)

    tm = x_ref.shape[0]
    x = _unpack_pair(x_ref[...])
    scores = _sigmoid(jnp.dot(x, w_ref[...], preferred_element_type=F32))
    lane = lax.broadcasted_iota(jnp.int32, (tm, LANES), 1)
    sel = jnp.where(lane < N_EXPERTS, scores + bias_ref[...], NEG_BIG)
    picks = []
    gates = []
    chosen = jnp.zeros((tm, LANES), F32)
    for _ in range(TOP_K):
        mx = jnp.max(sel, axis=-1, keepdims=True)
        idx = jnp.min(jnp.where(sel == mx, lane, LANES), axis=-1, keepdims=True)
        hit = lane == idx
        picks.append((idx, hit))
        gates.append(jnp.sum(jnp.where(hit, scores, 0.0), axis=-1, keepdims=True))
        chosen = chosen + hit.astype(F32)
        sel = jnp.where(hit, NEG_BIG, sel)
    gsum = gates[0]
    for gj in gates[1:]:
        gsum = gsum + gj
    before = jnp.dot(tri_ref[...], chosen.astype(BF16), preferred_element_type=F32) + carry_ref[...]
    idx_out = jnp.zeros((tm, LANES), jnp.int32)
    gate_out = jnp.zeros((tm, LANES), F32)
    rank_out = jnp.zeros((tm, LANES), jnp.int32)
    for j, (idx, hit) in enumerate(picks):
        rank = jnp.sum(jnp.where(hit, before, 0.0), axis=-1, keepdims=True).astype(jnp.int32)
        idx_out = jnp.where(lane == j, idx, idx_out)
        gate_out = jnp.where(lane == j, ROUTE_SCALE * gates[j] / gsum, gate_out)
        rank_out = jnp.where(lane == j, rank, rank_out)
    idx_o[...] = idx_out
    gate_o[...] = gate_out
    rank_o[...] = rank_out
    carry_ref[...] = carry_ref[...] + jnp.sum(chosen, axis=0, keepdims=True)
    cnt_o[...] = carry_ref[...].astype(jnp.int32)


def _router(vfp, router_w, bias):
    n = vfp.shape[0]
    tm = ROUTER_TILE if n % ROUTER_TILE == 0 else 128
    d = router_w.shape[0]
    wpad = jnp.zeros((d, LANES), BF16).at[:, :N_EXPERTS].set(router_w.astype(BF16))
    bpad = jnp.zeros((1, LANES), F32).at[0, :N_EXPERTS].set(bias)
    tri = (lax.broadcasted_iota(jnp.int32, (tm, tm), 1) < lax.broadcasted_iota(jnp.int32, (tm, tm), 0)).astype(BF16)
    row = pl.BlockSpec((tm, LANES), lambda i: (i, 0))
    return pl.pallas_call(
        _router_kernel,
        out_shape=(jax.ShapeDtypeStruct((n, LANES), jnp.int32), jax.ShapeDtypeStruct((n, LANES), F32),
                   jax.ShapeDtypeStruct((n, LANES), jnp.int32), jax.ShapeDtypeStruct((1, LANES), jnp.int32)),
        grid=(n // tm,),
        in_specs=[
            pl.BlockSpec((tm, d // 2), lambda i: (i, 0)),
            pl.BlockSpec((d, LANES), lambda i: (0, 0)),
            pl.BlockSpec((1, LANES), lambda i: (0, 0)),
            pl.BlockSpec((tm, tm), lambda i: (0, 0)),
        ],
        out_specs=(row, row, row, pl.BlockSpec((1, LANES), lambda i: (0, 0))),
        scratch_shapes=[pltpu.VMEM((1, LANES), F32)],
        compiler_params=_cparams("arbitrary"),
        name="moe_router",
    )(vfp, wpad, bpad, tri)


def _dispatch_kernel(dest_ref, x_ref, xs_in_ref, xs_ref, sem):
    del xs_in_ref
    tm = x_ref.shape[0]

    def copy(r, d):
        return pltpu.make_async_copy(x_ref.at[pl.ds(r, 1)], xs_ref.at[pl.ds(d, 1)], sem)

    def start(r, carry):
        for j in range(TOP_K):
            copy(r, dest_ref[0, 0, r * TOP_K + j]).start()
        return carry

    lax.fori_loop(0, tm, start, 0)

    def wait(r, carry):
        for _ in range(TOP_K):
            copy(0, 0).wait()
        return carry

    lax.fori_loop(0, tm, wait, 0)


def _dispatch(vfp, dest, n_rows):
    n, kw = vfp.shape
    tm = DISPATCH_TILE if n % DISPATCH_TILE == 0 else 128
    dest3 = dest.reshape(n // tm, 1, tm * TOP_K)
    xs0 = jnp.zeros((n_rows, kw), U32)
    return pl.pallas_call(
        _dispatch_kernel,
        out_shape=jax.ShapeDtypeStruct((n_rows, kw), U32),
        grid=(n // tm,),
        in_specs=[
            pl.BlockSpec((1, 1, tm * TOP_K), lambda i: (i, 0, 0), memory_space=pltpu.SMEM),
            pl.BlockSpec((tm, kw), lambda i: (i, 0)),
            pl.BlockSpec(memory_space=pl.ANY),
        ],
        out_specs=pl.BlockSpec(memory_space=pl.ANY),
        scratch_shapes=[pltpu.SemaphoreType.DMA(())],
        input_output_aliases={2: 0},
        compiler_params=_cparams("arbitrary"),
        name="moe_dispatch",
    )(dest3, vfp, xs0)


def _ffn_kernel(be_ref, on_ref, x_ref, w1_ref, w3_ref, w2_ref, o_ref, w1b, w3b, w2b):
    i = pl.program_id(0)
    prev = be_ref[jnp.maximum(i - 1, 0)]

    @pl.when((i == 0) | (be_ref[i] != prev))
    def _():
        w1b[...] = w1_ref[0].astype(BF16)
        w3b[...] = w3_ref[0].astype(BF16)
        w2b[...] = w2_ref[0].astype(BF16)

    @pl.when(on_ref[i] != 0)
    def _():
        x = _unpack_pair(x_ref[...])
        a = jnp.dot(x, w1b[...], preferred_element_type=F32)
        c = jnp.dot(x, w3b[...], preferred_element_type=F32)
        hmid = (a * _sigmoid(a) * c).astype(BF16)
        o_ref[...] = _pack_pair(jnp.dot(hmid, w2b[...], preferred_element_type=F32))

    @pl.when(on_ref[i] == 0)
    def _():
        o_ref[...] = jnp.zeros_like(o_ref)


def _grouped_ffn(xs, blk_e, blk_on, w1, w3, w2, tm):
    n_rows, kw = xs.shape
    d, ff = w1.shape[1], w1.shape[2]
    return pl.pallas_call(
        _ffn_kernel,
        out_shape=jax.ShapeDtypeStruct((n_rows, kw), U32),
        grid_spec=pltpu.PrefetchScalarGridSpec(
            num_scalar_prefetch=2,
            grid=(n_rows // tm,),
            in_specs=[
                pl.BlockSpec((tm, kw), lambda i, be, on: (i, 0)),
                pl.BlockSpec((1, d, ff), lambda i, be, on: (be[i], 0, 0)),
                pl.BlockSpec((1, d, ff), lambda i, be, on: (be[i], 0, 0)),
                pl.BlockSpec((1, ff, d), lambda i, be, on: (be[i], 0, 0)),
            ],
            out_specs=pl.BlockSpec((tm, kw), lambda i, be, on: (i, 0)),
            scratch_shapes=[pltpu.VMEM((d, ff), BF16), pltpu.VMEM((d, ff), BF16), pltpu.VMEM((ff, d), BF16)],
        ),
        compiler_params=_cparams("arbitrary"),
        name="moe_ffn",
    )(blk_e, blk_on, xs, w1, w3, w2)


def _combine_kernel(dest_ref, dnext_ref, gate_ref, sh_ref, h_ref, gf_ref, g_ref, b_ref, ys_ref, o_ref, buf_ref, sem):
    tm = h_ref.shape[0]
    i = pl.program_id(0)
    slot = i % 2

    def copy(sl, r, j, d):
        return pltpu.make_async_copy(ys_ref.at[pl.ds(d, 1)], buf_ref.at[sl, j, pl.ds(r, 1)], sem.at[sl])

    def start_tile(idx_ref, sl):
        def start(r, carry):
            for j in range(TOP_K):
                copy(sl, r, j, idx_ref[0, 0, r * TOP_K + j]).start()
            return carry

        lax.fori_loop(0, tm, start, 0)

    @pl.when(i == 0)
    def _():
        start_tile(dest_ref, 0)

    @pl.when(i + 1 < pl.num_programs(0))
    def _():
        start_tile(dnext_ref, 1 - slot)

    def wait(r, carry):
        for _ in range(TOP_K):
            copy(slot, 0, 0, 0).wait()
        return carry

    lax.fori_loop(0, tm, wait, 0)

    f = _unpack_pair(sh_ref[...]).astype(F32)
    gate = gate_ref[...]
    for j in range(TOP_K):
        f = f + gate[:, j:j + 1] * _unpack_pair(buf_ref[slot, j]).astype(F32)
    o_ref[...] = _layer_norm(DEEPNORM_ALPHA * h_ref[...] + gf_ref[0] * f, g_ref[...], b_ref[...])


def _combine(ys, dest, gate, shared, h, mod3, ln_g, ln_b, dims):
    n, d = h.shape
    kw = d // 2
    tm = COMBINE_TILE
    seq = _seq_of_tile(dims, tm)
    dest3 = dest.reshape(n // tm, 1, tm * TOP_K)
    return pl.pallas_call(
        _combine_kernel,
        out_shape=jax.ShapeDtypeStruct((n, d), F32),
        grid=(n // tm,),
        in_specs=[
            pl.BlockSpec((1, 1, tm * TOP_K), lambda i: (i, 0, 0), memory_space=pltpu.SMEM),
            pl.BlockSpec((1, 1, tm * TOP_K), lambda i: (jnp.minimum(i + 1, n // tm - 1), 0, 0),
                         memory_space=pltpu.SMEM),
            pl.BlockSpec((tm, LANES), lambda i: (i, 0)),
            pl.BlockSpec((tm, kw), lambda i: (i, 0)),
            pl.BlockSpec((tm, d), lambda i: (i, 0)),
            pl.BlockSpec((1, 1, d), lambda i: (seq(i) * 6 + 5, 0, 0)),
            pl.BlockSpec((1, d), lambda i: (0, 0)),
            pl.BlockSpec((1, d), lambda i: (0, 0)),
            pl.BlockSpec(memory_space=pl.ANY),
        ],
        out_specs=pl.BlockSpec((tm, d), lambda i: (i, 0)),
        scratch_shapes=[pltpu.VMEM((2, TOP_K, tm, kw), U32), pltpu.SemaphoreType.DMA((2,))],
        compiler_params=_cparams("arbitrary"),
        name="moe_combine_ln",
    )(dest3, dest3, gate, shared, h, mod3, ln_g.reshape(1, d), ln_b.reshape(1, d), ys)


def _moe(vfp, h, mod3, router_w, bias, w1, w3, w2, sw1, sw3, sw2, layer, ln_g, ln_b, dims):
    n = vfp.shape[0]
    tmb = MOE_BLOCK
    idx, gate, rank, cnt = _router(vfp, router_w, bias)
    counts = cnt[0, :N_EXPERTS]
    padded = (counts + tmb - 1) // tmb * tmb
    pend = jnp.cumsum(padded)
    pstart = pend - padded
    n_blocks = (n * TOP_K + N_EXPERTS * (tmb - 1) + tmb - 1) // tmb
    blk_pos = jnp.arange(n_blocks, dtype=jnp.int32) * tmb
    blk_e = jnp.minimum(jnp.sum((pend[None, :] <= blk_pos[:, None]).astype(jnp.int32), axis=1), N_EXPERTS - 1)
    blk_on = (blk_pos < pend[-1]).astype(jnp.int32)
    e_sel = idx[:, :TOP_K]
    onehot = e_sel[:, :, None] == jnp.arange(N_EXPERTS, dtype=jnp.int32)[None, None, :]
    dest = rank[:, :TOP_K] + jnp.sum(jnp.where(onehot, pstart[None, None, :], 0), axis=-1)
    dest = dest.reshape(-1).astype(jnp.int32)
    xs = _dispatch(vfp, dest, n_blocks * tmb)
    flat = lambda w: w.reshape((-1,) + w.shape[-2:])
    ys = _grouped_ffn(xs, blk_e + layer * N_EXPERTS, blk_on, flat(w1), flat(w3), flat(w2), tmb)
    ones = jnp.ones((n // tmb,), jnp.int32)
    shared = _grouped_ffn(vfp, jnp.full((n // tmb,), layer, jnp.int32), ones, sw1, sw3, sw2, tmb)
    return _combine(ys, dest, gate, shared, h, mod3, ln_g, ln_b, dims)


def _to_scan(rows, b):
    t = rows.shape[0] // b
    return rows.reshape(b, t, RW_HEADS, RW_HEAD).transpose(1, 3, 0, 2).reshape(t, RW_HEAD, b * RW_HEADS)


def _from_scan(y, b):
    t = y.shape[0]
    return y.reshape(t, RW_HEAD, b, RW_HEADS).transpose(2, 0, 3, 1).reshape(b * t, RW_WIDTH)


def _even_params(j, even_w_in, rw_mu, rw_w0, rw_w_up, rw_a0, rw_a_up, rw_g_up, rw_kk, rw_ka, rw_rk, rw_gn_g,
                 rw_gn_b, cv_w, cv_b, cv_ln_g, cv_ln_b):
    w3 = 3 * RW_WIDTH
    o_g, o_w, o_a = w3, w3 + GATE_LORA, w3 + GATE_LORA + 2 * DECAY_LORA
    o_cv = o_a + 2 * ICLR_LORA

    def relay(m):
        lead = m.shape[0]
        z = lambda k: jnp.zeros((lead, k), m.dtype)
        pieces = [m[:, :w3], m[:, o_g:o_g + GATE_LORA], z(LANES - GATE_LORA)]
        for d in range(2):
            pieces += [m[:, o_w + d * DECAY_LORA:o_w + (d + 1) * DECAY_LORA], z(LANES - DECAY_LORA)]
        for d in range(2):
            pieces += [m[:, o_a + d * ICLR_LORA:o_a + (d + 1) * ICLR_LORA], z(LANES - ICLR_LORA)]
        return jnp.concatenate(pieces, axis=1)

    w_in = even_w_in[j]
    w_pad = jnp.concatenate([relay(w_in), jnp.zeros((w_in.shape[0], 2 * 2048 - RW_STREAM_PAD), w_in.dtype),
                             w_in[:, o_cv:]], axis=1).astype(BF16)
    pad_rows = lambda m, k: jnp.concatenate([m, jnp.zeros((k - m.shape[0], m.shape[1]), m.dtype)], axis=0)
    head = jnp.arange(RW_WIDTH) // RW_HEAD
    vec = lambda m: m.reshape(1, -1)
    return {
        "w_in": w_pad,
        "mu": relay(rw_mu[j]),
        "w0": rw_w0[j], "a0": rw_a0[j],
        "w_up": jnp.stack([pad_rows(rw_w_up[j, d], LANES) for d in range(2)]).astype(BF16),
        "a_up": jnp.stack([pad_rows(rw_a_up[j, d], LANES) for d in range(2)]).astype(BF16),
        "g_up": pad_rows(rw_g_up[j], LANES).astype(BF16),
        "kk": vec(rw_kk[j]), "ka": vec(rw_ka[j]), "rk": vec(rw_rk[j]),
        "hs": (head[:, None] == head[None, :]).astype(BF16),
        "gn_g": vec(rw_gn_g[j]), "gn_b": vec(rw_gn_b[j]),
        "cv_w": pad_rows(cv_w[j], 32), "cv_b": vec(cv_b[j]), "cv_g": vec(cv_ln_g[j]), "cv_beta": vec(cv_ln_b[j]),
    }


def _even_mixer(h, mod3, prm, dims):
    p = _inproj(h, mod3, prm["w_in"], dims, 1, 0)
    r, v, kk, g, bon, dec, bvec, kd = _rwkv_features(p, prm, dims)
    b = dims[0]
    nl = b * dims[1]
    state = [jnp.zeros((RW_HEAD, RW_HEAD, LANES), F32)] * 2
    y_parts = []
    for rows in (slice(nl, None), slice(0, nl)):
        shared = [_to_scan(x[rows], b) for x in (r, kk, v)]
        ys = []
        for d in range(2):
            per_dir = [_to_scan(x[d, rows], b) for x in (dec, bvec, kd)]
            y_d, state[d] = _wkv_scan(*shared, *per_dir, state[d], reverse=(d == 1))
            ys.append(y_d)
        y_parts.append(_from_scan(ys[0] + ys[1], b))
    rw = _rwkv_post(jnp.concatenate([y_parts[1], y_parts[0]], axis=0), bon, g, prm)
    cv = _conformer_conv(p, prm, dims)
    return [rw, cv]


def _attn_mixer(h, mod3, w_in, qn, kn, tables, dims, with_ctx):
    p = _inproj(h, mod3, w_in, dims, 1, 0)
    q, k, v = _qkv_post(p, tables[0], tables[1], qn, kn, dims)
    out = _attention(q, k, v, dims, ctx_queries=False)
    if with_ctx:
        out_c = _attention(q, k, v, dims, ctx_queries=True)
    else:
        out_c = jnp.zeros((dims[0] * dims[2], ATT_Q), BF16)
    return [jnp.concatenate([out, out_c], axis=0)]


def kernel(x, c, ctx, c_ctx, ada_w, ada_b, ln1_g, ln1_b, ln2_g, ln2_b, even_w_in, even_w_out, rw_mu, rw_w0, rw_w_up, rw_a0, rw_a_up, rw_g_up, rw_kk, rw_ka, rw_rk, rw_gn_g, rw_gn_b, cv_w, cv_b, cv_ln_g, cv_ln_b, odd_w_in, odd_w_out, q_norm, k_norm, moe_router, moe_bias, moe_w1, moe_w3, moe_w2, sh_w1, sh_w3, sh_w2):
    bsz, s_len, d = x.shape
    ctx_len = ctx.shape[1]
    dims = (bsz, s_len, ctx_len)
    assert bsz * RW_HEADS == LANES, "the WKV scan maps (batch, head) pairs onto the 128 lanes"
    n_lat = bsz * s_len
    depth = ada_w.shape[0]

    c16 = jnp.zeros((16, d), F32).at[:bsz].set(c).at[bsz].set(c_ctx)
    mod_all = _ada_all(c16, ada_w, ada_b)[:, :bsz + 1].reshape(depth, (bsz + 1) * 6, 1, d)
    tables = _rope_tables(s_len, min(ROW_TILE, ctx_len))

    h = jnp.concatenate([x.reshape(n_lat, d), ctx.reshape(bsz * ctx_len, d)], axis=0)
    for layer in range(depth):
        j = layer // 2
        last = layer == depth - 1
        mod3 = mod_all[layer]
        if layer % 2 == 0:
            prm = _even_params(j, even_w_in, rw_mu, rw_w0, rw_w_up, rw_a0, rw_a_up, rw_g_up, rw_kk, rw_ka, rw_rk,
                               rw_gn_g, rw_gn_b, cv_w, cv_b, cv_ln_g, cv_ln_b)
            acts = _even_mixer(h, mod3, prm, dims)
            w_out = even_w_out[j].astype(BF16)
        else:
            acts = _attn_mixer(h, mod3, odd_w_in[j].astype(BF16), q_norm[j], k_norm[j], tables, dims,
                               with_ctx=not last)
            w_out = odd_w_out[j].astype(BF16)
        h, vfp = _outproj(acts, w_out, h, mod3, ln1_g[layer], ln1_b[layer], dims)
        h = _moe(vfp, h, mod3, moe_router[layer], moe_bias[layer], moe_w1, moe_w3, moe_w2, sh_w1, sh_w3, sh_w2,
                 layer, ln2_g[layer], ln2_b[layer], dims)
    return h[:n_lat].reshape(bsz, s_len, d)
```

```python
import functools
import math

import jax
import jax.numpy as jnp
from jax import lax
from jax.experimental import pallas as pl
from jax.experimental.pallas import tpu as pltpu

F32 = jnp.float32
BF16 = jnp.bfloat16
U32 = jnp.uint32

D_MODEL = 2048
DEPTH = 4
GRID_W = 64

RW_HEADS = 16
RW_HEAD = 64
RW_WIDTH = RW_HEADS * RW_HEAD
DECAY_LORA = 96
ICLR_LORA = 96
GATE_LORA = 64
RW_GN_EPS = 64e-5
CONV_WIDTH = 1024
CONV_K = 31

ATT_HEADS = 16
ATT_KV_HEADS = 4
ATT_HEAD = 128
ATT_GROUP = ATT_HEADS // ATT_KV_HEADS
ATT_Q = ATT_HEADS * ATT_HEAD
ATT_KV = ATT_KV_HEADS * ATT_HEAD
ROPE_THETA = 10000.0
ROPE_PAIRS = ATT_HEAD // 4
QK_EPS = 1e-6

N_EXPERTS = 64
TOP_K = 6
EXPERT_FF = 384
ROUTE_SCALE = 2.5

DEEPNORM_ALPHA = (2 * DEPTH) ** 0.25
LN_EPS = 1e-5

LANES = 128
SMALL_W = 640
RW_STREAM_PAD = 3 * RW_WIDTH + SMALL_W
EVEN_IN_PAD = 6144
CONV_COL_BLOCK = 2
EXP_M05 = math.exp(-0.5)
NEG_BIG = -1e30
VMEM_LIMIT = 56 * 1024 * 1024

ROW_TILE = 256
HALO = 16
SCAN_STEPS = 32
MOE_BLOCK = 512
ROUTER_TILE = 512
DISPATCH_TILE = 256
COMBINE_TILE = 128
ATTN_ROW_BLOCK = 128


def _cparams(*sem):
    return pltpu.CompilerParams(dimension_semantics=tuple(sem), vmem_limit_bytes=VMEM_LIMIT)


def _layer_norm(x, g, b):
    mu = jnp.mean(x, axis=-1, keepdims=True)
    xc = x - mu
    var = jnp.mean(xc * xc, axis=-1, keepdims=True)
    return xc * lax.rsqrt(var + LN_EPS) * g + b


def _sigmoid(x):
    return 1.0 / (1.0 + jnp.exp(-x))


def _pack_pair(x):
    k = x.shape[1] // 2
    lo = lax.bitcast_convert_type(x[:, :k].astype(BF16).astype(F32), U32) >> 16
    hi = lax.bitcast_convert_type(x[:, k:].astype(BF16).astype(F32), U32) & jnp.uint32(0xFFFF0000)
    return hi | lo


def _unpack_pair(w):
    lo = lax.bitcast_convert_type(w << 16, F32)
    hi = lax.bitcast_convert_type(w & jnp.uint32(0xFFFF0000), F32)
    return jnp.concatenate([lo, hi], axis=1).astype(BF16)


def _col_tile(n, cap=1280):
    best = LANES
    for t in range(LANES, cap + 1, LANES):
        if n % t == 0:
            best = t
    return best


def _ada_kernel(c_ref, w_ref, b_ref, o_ref):
    c = c_ref[...]
    s = (c * _sigmoid(c)).astype(BF16)
    o_ref[0] = jnp.dot(s, w_ref[0].astype(BF16), preferred_element_type=F32) + b_ref[0]


def _ada_all(c16, ada_w, ada_b):
    depth, d, n6 = ada_w.shape
    tn = 1024
    return pl.pallas_call(
        _ada_kernel,
        out_shape=jax.ShapeDtypeStruct((depth, 16, n6), F32),
        grid=(depth, n6 // tn),
        in_specs=[
            pl.BlockSpec((16, d), lambda l, j: (0, 0)),
            pl.BlockSpec((1, d, tn), lambda l, j: (l, 0, j)),
            pl.BlockSpec((1, 1, tn), lambda l, j: (l, 0, j)),
        ],
        out_specs=pl.BlockSpec((1, 16, tn), lambda l, j: (l, 0, j)),
        compiler_params=_cparams("parallel", "parallel"),
        name="ada_mod",
    )(c16, ada_w, ada_b.reshape(depth, 1, n6))


def _inproj_kernel(x_ref, sc_ref, sh_ref, w_ref, o_ref, u_ref):
    @pl.when(pl.program_id(1) == 0)
    def _():
        u_ref[...] = (x_ref[...] * (1.0 + sc_ref[0]) + sh_ref[0]).astype(BF16)

    o_ref[...] = jnp.dot(u_ref[...], w_ref[...], preferred_element_type=F32).astype(o_ref.dtype)


def _inproj(h, mod3, w, dims, sc_idx, sh_idx):
    n, d = h.shape
    nout = w.shape[1]
    tm = min(512, dims[1])
    tn = _col_tile(nout)
    seq = _seq_of_tile(dims, tm)
    return pl.pallas_call(
        _inproj_kernel,
        out_shape=jax.ShapeDtypeStruct((n, nout), BF16),
        grid=(n // tm, nout // tn),
        in_specs=[
            pl.BlockSpec((tm, d), lambda i, j: (i, 0)),
            pl.BlockSpec((1, 1, d), lambda i, j: (seq(i) * 6 + sc_idx, 0, 0)),
            pl.BlockSpec((1, 1, d), lambda i, j: (seq(i) * 6 + sh_idx, 0, 0)),
            pl.BlockSpec((d, tn), lambda i, j: (0, j)),
        ],
        out_specs=pl.BlockSpec((tm, tn), lambda i, j: (i, j)),
        scratch_shapes=[pltpu.VMEM((tm, d), BF16)],
        compiler_params=_cparams("parallel", "arbitrary"),
        name="inproj",
    )(h, mod3, mod3, w)


def _seq_of_tile(dims, tm):
    b, s, _ = dims

    def seq(i):
        return jnp.minimum((i * tm) // s, b)

    return seq


def _outproj_kernel(*refs, splits):
    n_in = len(splits)
    a_refs = refs[:n_in]
    w_ref, h_ref, ga_ref, scf_ref, shf_ref, g_ref, b_ref, ho_ref, vf_ref = refs[n_in:]
    acc = None
    off = 0
    for a_ref, k in zip(a_refs, splits):
        part = jnp.dot(a_ref[...], w_ref[off:off + k, :], preferred_element_type=F32)
        acc = part if acc is None else acc + part
        off += k
    hn = _layer_norm(DEEPNORM_ALPHA * h_ref[...] + ga_ref[0] * acc, g_ref[...], b_ref[...])
    ho_ref[...] = hn
    vf_ref[...] = _pack_pair(hn * (1.0 + scf_ref[0]) + shf_ref[0])


def _outproj(acts, w, h, mod3, ln_g, ln_b, dims):
    n, d = h.shape
    tm = min(256, dims[1])
    seq = _seq_of_tile(dims, tm)
    splits = tuple(a.shape[1] for a in acts)
    in_specs = [pl.BlockSpec((tm, k), lambda i: (i, 0)) for k in splits]
    in_specs += [
        pl.BlockSpec(w.shape, lambda i: (0, 0)),
        pl.BlockSpec((tm, d), lambda i: (i, 0)),
        pl.BlockSpec((1, 1, d), lambda i: (seq(i) * 6 + 2, 0, 0)),
        pl.BlockSpec((1, 1, d), lambda i: (seq(i) * 6 + 4, 0, 0)),
        pl.BlockSpec((1, 1, d), lambda i: (seq(i) * 6 + 3, 0, 0)),
        pl.BlockSpec((1, d), lambda i: (0, 0)),
        pl.BlockSpec((1, d), lambda i: (0, 0)),
    ]
    return pl.pallas_call(
        functools.partial(_outproj_kernel, splits=splits),
        out_shape=(jax.ShapeDtypeStruct((n, d), F32), jax.ShapeDtypeStruct((n, d // 2), U32)),
        grid=(n // tm,),
        in_specs=in_specs,
        out_specs=(pl.BlockSpec((tm, d), lambda i: (i, 0)), pl.BlockSpec((tm, d // 2), lambda i: (i, 0))),
        compiler_params=_cparams("parallel"),
        name="outproj_ln",
    )(*acts, w, h, mod3, mod3, mod3, ln_g.reshape(1, d), ln_b.reshape(1, d))


def _tile_flags(dims, tb):
    b, s, ctx = dims
    n = b * (s + ctx)
    nlt = (b * s) // tb
    nbl = s // tb
    nbc = ctx // tb

    def first_last(i):
        il = i % nbl
        ic = (i - nlt) % nbc
        lat = i < nlt
        first = jnp.where(lat, il == 0, ic == 0)
        last = jnp.where(lat, il == nbl - 1, ic == nbc - 1)
        return first, last

    def prev_blk(i):
        return jnp.maximum((i * tb) // HALO - 1, 0)

    def next_blk(i):
        return jnp.minimum(((i + 1) * tb) // HALO, n // HALO - 1)

    return first_last, prev_blk, next_blk


def _head_sum(x, hs_ref):
    hi = x.astype(BF16)
    lo = (x - hi.astype(F32)).astype(BF16)
    hs = hs_ref[...]
    return jnp.dot(hi, hs, preferred_element_type=F32) + jnp.dot(lo, hs, preferred_element_type=F32)


def _feat_kernel(p_ref, pp_ref, pn_ref, mu_ref, w0_ref, a0_ref, wup_ref, aup_ref, gup_ref, kk_ref, ka_ref,
                 rk_ref, hs_ref, r_o, v_o, kkn_o, g_o, bon_o, dec_o, b_o, kd_o, buf_ref, *, tb, first_last):
    first, last = first_last(pl.program_id(0))
    w = RW_WIDTH
    buf_ref[8:8 + tb, :] = p_ref[...].astype(F32)
    buf_ref[7:8, :] = jnp.where(first, 0.0, pp_ref[HALO - 1:HALO, :].astype(F32))
    buf_ref[8 + tb:9 + tb, :] = jnp.where(last, 0.0, pn_ref[0:1, :].astype(F32))

    def shifted(c0, c1):
        cur = buf_ref[8:8 + tb, c0:c1]
        prev = buf_ref[7:7 + tb, c0:c1]
        nxt = buf_ref[9:9 + tb, c0:c1]
        return cur + mu_ref[0:1, c0:c1] * (prev - cur) + mu_ref[1:2, c0:c1] * (nxt - cur)

    r = shifted(0, w)
    k = shifted(w, 2 * w)
    v = shifted(2 * w, 3 * w)
    sm = shifted(3 * w, 3 * w + SMALL_W)

    g = jnp.dot(_sigmoid(sm[:, 0:LANES]).astype(BF16), gup_ref[...], preferred_element_type=F32)
    kkr = k * kk_ref[...]
    kk = kkr * lax.rsqrt(_head_sum(kkr * kkr, hs_ref) + 1e-12)

    ksum = None
    for d in range(2):
        wd = sm[:, LANES * (1 + d):LANES * (2 + d)]
        ad = sm[:, LANES * (3 + d):LANES * (4 + d)]
        xw = w0_ref[d:d + 1, :] + jnp.dot(jnp.tanh(wd).astype(BF16), wup_ref[d], preferred_element_type=F32)
        dec_o[d] = jnp.exp(-EXP_M05 * _sigmoid(xw))
        a = _sigmoid(a0_ref[d:d + 1, :] + jnp.dot(ad.astype(BF16), aup_ref[d], preferred_element_type=F32))
        b_o[d] = (a * kk).astype(BF16)
        kd = k * (1.0 + (a - 1.0) * ka_ref[...])
        kd_o[d] = kd.astype(BF16)
        ksum = kd if ksum is None else ksum + kd

    bon_o[...] = (_head_sum(r * ksum * rk_ref[...], hs_ref) * v).astype(BF16)
    r_o[...] = r.astype(BF16)
    v_o[...] = v.astype(BF16)
    kkn_o[...] = kk.astype(BF16)
    g_o[...] = g.astype(BF16)


def _rwkv_features(p, prm, dims):
    n = p.shape[0]
    tb = min(ROW_TILE, dims[2])
    first_last, prev_blk, next_blk = _tile_flags(dims, tb)
    w = RW_WIDTH
    pw = RW_STREAM_PAD
    full2 = lambda shape: pl.BlockSpec(shape, lambda i: (0, 0))
    full3 = lambda shape: pl.BlockSpec(shape, lambda i: (0, 0, 0))
    row = pl.BlockSpec((tb, w), lambda i: (i, 0))
    row2 = pl.BlockSpec((2, tb, w), lambda i: (0, i, 0))
    sd = lambda dt: jax.ShapeDtypeStruct((n, w), dt)
    sd2 = lambda dt: jax.ShapeDtypeStruct((2, n, w), dt)
    return pl.pallas_call(
        functools.partial(_feat_kernel, tb=tb, first_last=first_last),
        out_shape=(sd(BF16), sd(BF16), sd(BF16), sd(BF16), sd(BF16), sd2(F32), sd2(BF16), sd2(BF16)),
        grid=(n // tb,),
        in_specs=[
            pl.BlockSpec((tb, pw), lambda i: (i, 0)),
            pl.BlockSpec((HALO, pw), lambda i: (prev_blk(i), 0)),
            pl.BlockSpec((HALO, pw), lambda i: (next_blk(i), 0)),
            full2((2, pw)), full2((2, w)), full2((2, w)),
            full3((2, LANES, w)), full3((2, LANES, w)), full2((LANES, w)),
            full2((1, w)), full2((1, w)), full2((1, w)), full2((w, w)),
        ],
        out_specs=(row, row, row, row, row, row2, row2, row2),
        scratch_shapes=[pltpu.VMEM((tb + 16, pw), F32)],
        compiler_params=_cparams("parallel"),
        name="rwkv_features",
    )(p, p, p, prm["mu"], prm["w0"], prm["a0"], prm["w_up"], prm["a_up"], prm["g_up"], prm["kk"], prm["ka"],
      prm["rk"], prm["hs"])


def _wkv_kernel(r_ref, kk_ref, v_ref, w_ref, b_ref, kd_ref, s0_ref, *refs, steps, reverse, add_prev):
    yp_ref = refs[0] if add_prev else None
    y_ref, st_ref, s_ref, rf, kkf, vf, bf, kdf = refs[1:] if add_prev else refs
    i = pl.program_id(0)

    @pl.when(i == 0)
    def _():
        s_ref[...] = s0_ref[...]

    for src, dst in ((r_ref, rf), (kk_ref, kkf), (v_ref, vf), (b_ref, bf), (kd_ref, kdf)):
        dst[...] = src[...].astype(F32)

    nk = RW_HEAD
    half = nk // 2
    t_first = steps - 1 if reverse else 0

    def s_dot_kk(t, lo):
        acc = jnp.zeros((half, LANES), F32)
        for k in range(nk):
            acc = acc + s_ref[k, lo:lo + half, :] * kkf[t, pl.ds(k, 1), :]
        return acc

    def step(j, sks):
        t = steps - 1 - j if reverse else j
        t_next = jnp.maximum(t - 1, 0) if reverse else jnp.minimum(t + 1, steps - 1)
        sks_next = []
        for sk, lo in zip(sks, (0, half)):
            vt = vf[t, lo:lo + half, :]
            y = jnp.zeros((half, LANES), F32)
            sk_next = jnp.zeros((half, LANES), F32)
            for k in range(nk):
                s = (s_ref[k, lo:lo + half, :] * w_ref[t, pl.ds(k, 1), :] - sk * bf[t, pl.ds(k, 1), :]
                     + vt * kdf[t, pl.ds(k, 1), :])
                s_ref[k, lo:lo + half, :] = s
                y = y + s * rf[t, pl.ds(k, 1), :]
                sk_next = sk_next + s * kkf[t_next, pl.ds(k, 1), :]
            y_ref[t, lo:lo + half, :] = y + yp_ref[t, lo:lo + half, :] if add_prev else y
            sks_next.append(sk_next)
        return tuple(sks_next)

    lax.fori_loop(0, steps, step, (s_dot_kk(t_first, 0), s_dot_kk(t_first, half)))

    @pl.when(i == pl.num_programs(0) - 1)
    def _():
        st_ref[...] = s_ref[...]


def _wkv_scan(r, kk, v, w, b, kd, s0, reverse, y_prev=None):
    t_all = r.shape[0]
    extra = [] if y_prev is None else [y_prev]
    steps = min(SCAN_STEPS, t_all)
    nblk = t_all // steps
    blk = (lambda i: (nblk - 1 - i, 0, 0)) if reverse else (lambda i: (i, 0, 0))
    seq = pl.BlockSpec((steps, RW_HEAD, LANES), blk)
    state = pl.BlockSpec((RW_HEAD, RW_HEAD, LANES), lambda i: (0, 0, 0))
    buf = pltpu.VMEM((steps, RW_HEAD, LANES), F32)
    return pl.pallas_call(
        functools.partial(_wkv_kernel, steps=steps, reverse=reverse, add_prev=y_prev is not None),
        out_shape=(jax.ShapeDtypeStruct((t_all, RW_HEAD, LANES), F32),
                   jax.ShapeDtypeStruct((RW_HEAD, RW_HEAD, LANES), F32)),
        grid=(nblk,),
        in_specs=[seq, seq, seq, seq, seq, seq, state] + [seq] * len(extra),
        out_specs=(seq, state),
        scratch_shapes=[pltpu.VMEM((RW_HEAD, RW_HEAD, LANES), F32), buf, buf, buf, buf, buf],
        compiler_params=_cparams("arbitrary"),
        name="wkv_scan_bwd" if reverse else "wkv_scan_fwd",
    )(r, kk, v, w, b, kd, s0, *extra)


def _rwpost_kernel(y_ref, bon_ref, g_ref, gng_ref, gnb_ref, hs_ref, o_ref):
    y = y_ref[...]
    inv = 1.0 / RW_HEAD
    yc = y - _head_sum(y, hs_ref) * inv
    var = _head_sum(yc * yc, hs_ref) * inv
    yn = yc * lax.rsqrt(var + RW_GN_EPS) * gng_ref[...] + gnb_ref[...]
    o_ref[...] = ((yn + bon_ref[...].astype(F32)) * g_ref[...].astype(F32)).astype(BF16)


def _rwkv_post(y, bon, g, prm):
    n, w = y.shape
    tb = 256
    row = pl.BlockSpec((tb, w), lambda i: (i, 0))
    vec = pl.BlockSpec((1, w), lambda i: (0, 0))
    return pl.pallas_call(
        _rwpost_kernel,
        out_shape=jax.ShapeDtypeStruct((n, w), BF16),
        grid=(n // tb,),
        in_specs=[row, row, row, vec, vec, pl.BlockSpec((w, w), lambda i: (0, 0))],
        out_specs=row,
        compiler_params=_cparams("parallel"),
        name="rwkv_post",
    )(y, bon, g, prm["gn_g"], prm["gn_b"], prm["hs"])


def _conv_kernel(p_ref, pp_ref, pn_ref, w_ref, b_ref, g_ref, beta_ref, o_ref, z_ref, *, tb, first_last):
    first, last = first_last(pl.program_id(0))
    c = CONV_WIDTH

    def glu(x):
        x = x.astype(F32)
        return x[:, :c] * _sigmoid(x[:, c:])

    z_ref[HALO:HALO + tb, :] = glu(p_ref[...])
    z_ref[0:HALO, :] = jnp.where(first, 0.0, glu(pp_ref[...]))
    z_ref[HALO + tb:2 * HALO + tb, :] = jnp.where(last, 0.0, glu(pn_ref[...]))
    acc = jnp.zeros((tb, c), F32)
    base = HALO - CONV_K // 2
    for j in range(CONV_K):
        acc = acc + z_ref[base + j:base + j + tb, :] * w_ref[j:j + 1, :]
    zn = _layer_norm(acc + b_ref[...], g_ref[...], beta_ref[...])
    o_ref[...] = (zn * _sigmoid(zn)).astype(BF16)


def _conformer_conv(p, prm, dims):
    n = p.shape[0]
    tb = min(ROW_TILE, dims[2])
    first_last, prev_blk, next_blk = _tile_flags(dims, tb)
    c = CONV_WIDTH
    cb = CONV_COL_BLOCK
    vec = pl.BlockSpec((1, c), lambda i: (0, 0))
    return pl.pallas_call(
        functools.partial(_conv_kernel, tb=tb, first_last=first_last),
        out_shape=jax.ShapeDtypeStruct((n, c), BF16),
        grid=(n // tb,),
        in_specs=[
            pl.BlockSpec((tb, 2 * c), lambda i: (i, cb)),
            pl.BlockSpec((HALO, 2 * c), lambda i: (prev_blk(i), cb)),
            pl.BlockSpec((HALO, 2 * c), lambda i: (next_blk(i), cb)),
            pl.BlockSpec((32, c), lambda i: (0, 0)), vec, vec, vec,
        ],
        out_specs=pl.BlockSpec((tb, c), lambda i: (i, 0)),
        scratch_shapes=[pltpu.VMEM((tb + 2 * HALO, c), F32)],
        compiler_params=_cparams("parallel"),
        name="conformer_conv",
    )(p, p, p, prm["cv_w"], prm["cv_b"], prm["cv_g"], prm["cv_beta"])


def _qkv_kernel(p_ref, cos_ref, sin_ref, qn_ref, kn_ref, q_o, k_o, v_o):
    lane = lax.broadcasted_iota(jnp.int32, (1, ATT_HEAD), 1)
    first_half = (lane % 64) < 32
    cosv = cos_ref[...]
    sinv = sin_ref[...]

    def norm_rope(x, gain, scale):
        x = x.astype(F32)
        xn = x * lax.rsqrt(jnp.mean(x * x, axis=-1, keepdims=True) + QK_EPS) * gain
        swapped = jnp.where(first_half, pltpu.roll(xn, ATT_HEAD - 32, axis=1), pltpu.roll(xn, 32, axis=1))
        return (xn * cosv + swapped * sinv) * scale

    hd = ATT_HEAD
    ones = jnp.ones((p_ref.shape[0], hd), BF16)
    for h in range(ATT_HEADS):
        q_o[h] = norm_rope(p_ref[:, h * hd:(h + 1) * hd], qn_ref[...], 1.0 / math.sqrt(ATT_HEAD)).astype(BF16)
    for h in range(ATT_KV_HEADS):
        k_o[h] = norm_rope(p_ref[:, ATT_Q + h * hd:ATT_Q + (h + 1) * hd], kn_ref[...], 1.0).T.astype(BF16)
        v_o[h, :, :hd] = p_ref[:, ATT_Q + ATT_KV + h * hd:ATT_Q + ATT_KV + (h + 1) * hd]
        v_o[h, :, hd:] = ones


def _qkv_post(p, cos_t, sin_t, qn, kn, dims):
    b, s, ctx = dims
    n = p.shape[0]
    tb = min(ROW_TILE, ctx)
    nlt = (b * s) // tb
    nbl = s // tb
    hd = ATT_HEAD

    def tab(i):
        return jnp.where(i < nlt, i % nbl, nbl)

    return pl.pallas_call(
        _qkv_kernel,
        out_shape=(jax.ShapeDtypeStruct((ATT_HEADS, n, hd), BF16),
                   jax.ShapeDtypeStruct((ATT_KV_HEADS, hd, n), BF16),
                   jax.ShapeDtypeStruct((ATT_KV_HEADS, n, 2 * hd), BF16)),
        grid=(n // tb,),
        in_specs=[
            pl.BlockSpec((tb, p.shape[1]), lambda i: (i, 0)),
            pl.BlockSpec((tb, hd), lambda i: (tab(i), 0)),
            pl.BlockSpec((tb, hd), lambda i: (tab(i), 0)),
            pl.BlockSpec((1, hd), lambda i: (0, 0)),
            pl.BlockSpec((1, hd), lambda i: (0, 0)),
        ],
        out_specs=(pl.BlockSpec((ATT_HEADS, tb, hd), lambda i: (0, i, 0)),
                   pl.BlockSpec((ATT_KV_HEADS, hd, tb), lambda i: (0, 0, i)),
                   pl.BlockSpec((ATT_KV_HEADS, tb, 2 * hd), lambda i: (0, i, 0))),
        compiler_params=_cparams("parallel"),
        name="qkv_norm_rope",
    )(p, cos_t, sin_t, qn.reshape(1, hd), kn.reshape(1, hd))


def _rope_tables(s, pad_rows):
    rows_n = s // GRID_W
    row = jnp.repeat(jnp.arange(rows_n), GRID_W)
    col = jnp.tile(jnp.arange(GRID_W), rows_n)
    inv = ROPE_THETA ** (-jnp.arange(ROPE_PAIRS, dtype=F32) / ROPE_PAIRS)
    ang_r = row[:, None] * inv
    ang_c = col[:, None] * inv
    cos_t = jnp.concatenate([jnp.cos(ang_r), jnp.cos(ang_r), jnp.cos(ang_c), jnp.cos(ang_c)], axis=1)
    sin_t = jnp.concatenate([-jnp.sin(ang_r), jnp.sin(ang_r), -jnp.sin(ang_c), jnp.sin(ang_c)], axis=1)
    cos_t = jnp.concatenate([cos_t, jnp.ones((pad_rows, ATT_HEAD), F32)], axis=0)
    sin_t = jnp.concatenate([sin_t, jnp.zeros((pad_rows, ATT_HEAD), F32)], axis=0)
    return cos_t, sin_t


def _attn_kernel(q_ref, *refs, seg_lens, kc):
    n_seg = len(seg_lens)
    o_ref, s_ref, mx_ref, p_ref = refs[2 * n_seg:]
    g, tq, hd = q_ref.shape

    @pl.when(pl.program_id(0) == 0)
    def _():
        p_ref[...] = jnp.zeros_like(p_ref)

    acc = None
    off = 0
    for si, n_keys in enumerate(seg_lens):
        pv = jnp.dot(p_ref[:, off:off + n_keys], refs[2 * si + 1][0], preferred_element_type=F32)
        acc = pv if acc is None else acc + pv
        off += n_keys

    q = q_ref[...].reshape(g * tq, hd)
    chunks = []
    off = 0
    for si, n_keys in enumerate(seg_lens):
        for c0 in range(0, n_keys, kc):
            c = min(kc, n_keys - c0)
            chunks.append((si, c0, c, off))
            off += c
    for n, (si, c0, c, off) in enumerate(chunks):
        s = jnp.dot(q, refs[2 * si][0, :, c0:c0 + c], preferred_element_type=F32)
        s_ref[:, off:off + c] = s
        part = s[:, :LANES]
        for j in range(1, c // LANES):
            part = jnp.maximum(part, s[:, j * LANES:(j + 1) * LANES])
        mx_ref[...] = part if n == 0 else jnp.maximum(mx_ref[...], part)

    o = acc[:, :hd] / jnp.maximum(acc[:, hd:], 1e-30)
    for gi in range(g):
        o_ref[:, gi * hd:(gi + 1) * hd] = o[gi * tq:(gi + 1) * tq, :].astype(o_ref.dtype)

    mx_ref[...] = jnp.broadcast_to(jnp.max(mx_ref[...], axis=-1, keepdims=True), mx_ref.shape)
    rb = min(ATTN_ROW_BLOCK, g * tq)

    def exp_rows(i, carry):
        r0 = pl.multiple_of(i * rb, rb)
        mb = mx_ref[pl.ds(r0, rb), :]
        for _, _, c, off in chunks:
            e = jnp.exp(s_ref[pl.ds(r0, rb), off:off + c] - jnp.concatenate([mb] * (c // LANES), axis=1))
            p_ref[pl.ds(r0, rb), off:off + c] = e.astype(BF16)
        return carry

    lax.fori_loop(0, (g * tq) // rb, exp_rows, 0)


def _attention(q, kt, v1, dims, ctx_queries):
    b, s, ctx = dims
    hd = ATT_HEAD
    g = ATT_GROUP
    kvh = ATT_KV_HEADS
    nl = b * s
    if ctx_queries:
        tq, nq, rows, q_base = ctx, 1, b * ctx, nl // ctx
        segs = [(ctx, nl // ctx)]
    else:
        tq = min(256, s)
        nq, rows, q_base = s // tq, nl, 0
        segs = [(s, 0), (ctx, nl // ctx)]
    n_tiles = b * kvh * nq

    def split(tile):
        return tile // (kvh * nq), (tile // nq) % kvh, tile % nq

    cur = lambda t: split(jnp.minimum(t, n_tiles - 1))
    prev = lambda t: split(jnp.maximum(t - 1, 0))

    def q_map(t):
        bi, kv, qi = cur(t)
        return kv, q_base + bi * nq + qi, 0

    def o_map(t):
        bi, kv, qi = prev(t)
        return bi * nq + qi, kv

    in_specs = [pl.BlockSpec((g, tq, hd), q_map)]
    args = [q]
    for n_keys, base in segs:
        def k_map(t, base=base):
            bi, kv, _ = cur(t)
            return kv, 0, base + bi

        def v_map(t, base=base):
            bi, kv, _ = prev(t)
            return kv, base + bi, 0

        in_specs += [pl.BlockSpec((1, hd, n_keys), k_map), pl.BlockSpec((1, n_keys, 2 * hd), v_map)]
        args += [kt, v1]
    n_all = sum(n for n, _ in segs)
    return pl.pallas_call(
        functools.partial(_attn_kernel, seg_lens=tuple(n for n, _ in segs), kc=512),
        out_shape=jax.ShapeDtypeStruct((rows, ATT_Q), BF16),
        grid=(n_tiles + 1,),
        in_specs=in_specs,
        out_specs=pl.BlockSpec((tq, g * hd), o_map),
        scratch_shapes=[pltpu.VMEM((g * tq, n_all), F32), pltpu.VMEM((g * tq, LANES), F32),
                        pltpu.VMEM((g * tq, n_all), BF16)],
        compiler_params=_cparams("arbitrary"),
        name="gqa_attention_ctx" if ctx_queries else "gqa_attention",
    )(*args)


def _router_kernel(x_ref, w_ref, bias_ref, tri_ref, idx_o, gate_o, rank_o, cnt_o, carry_ref):
    @pl.when(pl.program_id(0) == 0)
    def _():
        carry_ref[...] = jnp.zeros_like(carry_ref)

    tm = x_ref.shape[0]
    x = _unpack_pair(x_ref[...])
    scores = _sigmoid(jnp.dot(x, w_ref[...], preferred_element_type=F32))
    lane = lax.broadcasted_iota(jnp.int32, (tm, LANES), 1)
    sel = jnp.where(lane < N_EXPERTS, scores + bias_ref[...], NEG_BIG)
    picks = []
    gates = []
    chosen = jnp.zeros((tm, LANES), F32)
    for _ in range(TOP_K):
        mx = jnp.max(sel, axis=-1, keepdims=True)
        idx = jnp.min(jnp.where(sel == mx, lane, LANES), axis=-1, keepdims=True)
        hit = lane == idx
        picks.append((idx, hit))
        gates.append(jnp.sum(jnp.where(hit, scores, 0.0), axis=-1, keepdims=True))
        chosen = chosen + hit.astype(F32)
        sel = jnp.where(hit, NEG_BIG, sel)
    gsum = gates[0]
    for gj in gates[1:]:
        gsum = gsum + gj
    before = jnp.dot(tri_ref[...], chosen.astype(BF16), preferred_element_type=F32) + carry_ref[...]
    idx_out = jnp.zeros((tm, LANES), jnp.int32)
    gate_out = jnp.zeros((tm, LANES), F32)
    rank_out = jnp.zeros((tm, LANES), jnp.int32)
    for j, (idx, hit) in enumerate(picks):
        rank = jnp.sum(jnp.where(hit, before, 0.0), axis=-1, keepdims=True).astype(jnp.int32)
        idx_out = jnp.where(lane == j, idx, idx_out)
        gate_out = jnp.where(lane == j, ROUTE_SCALE * gates[j] / gsum, gate_out)
        rank_out = jnp.where(lane == j, rank, rank_out)
    idx_o[...] = idx_out
    gate_o[...] = gate_out
    rank_o[...] = rank_out
    carry_ref[...] = carry_ref[...] + jnp.sum(chosen, axis=0, keepdims=True)
    cnt_o[...] = carry_ref[...].astype(jnp.int32)


def _router(vfp, router_w, bias):
    n = vfp.shape[0]
    tm = ROUTER_TILE if n % ROUTER_TILE == 0 else 128
    d = router_w.shape[0]
    wpad = jnp.zeros((d, LANES), BF16).at[:, :N_EXPERTS].set(router_w.astype(BF16))
    bpad = jnp.zeros((1, LANES), F32).at[0, :N_EXPERTS].set(bias)
    tri = (lax.broadcasted_iota(jnp.int32, (tm, tm), 1) < lax.broadcasted_iota(jnp.int32, (tm, tm), 0)).astype(BF16)
    row = pl.BlockSpec((tm, LANES), lambda i: (i, 0))
    return pl.pallas_call(
        _router_kernel,
        out_shape=(jax.ShapeDtypeStruct((n, LANES), jnp.int32), jax.ShapeDtypeStruct((n, LANES), F32),
                   jax.ShapeDtypeStruct((n, LANES), jnp.int32), jax.ShapeDtypeStruct((1, LANES), jnp.int32)),
        grid=(n // tm,),
        in_specs=[
            pl.BlockSpec((tm, d // 2), lambda i: (i, 0)),
            pl.BlockSpec((d, LANES), lambda i: (0, 0)),
            pl.BlockSpec((1, LANES), lambda i: (0, 0)),
            pl.BlockSpec((tm, tm), lambda i: (0, 0)),
        ],
        out_specs=(row, row, row, pl.BlockSpec((1, LANES), lambda i: (0, 0))),
        scratch_shapes=[pltpu.VMEM((1, LANES), F32)],
        compiler_params=_cparams("arbitrary"),
        name="moe_router",
    )(vfp, wpad, bpad, tri)


def _dispatch_kernel(dest_ref, x_ref, xs_in_ref, xs_ref, sem):
    del xs_in_ref
    tm = x_ref.shape[0]

    def copy(r, d):
        return pltpu.make_async_copy(x_ref.at[pl.ds(r, 1)], xs_ref.at[pl.ds(d, 1)], sem)

    def start(r, carry):
        for j in range(TOP_K):
            copy(r, dest_ref[0, 0, r * TOP_K + j]).start(priority=j % 2)
        return carry

    lax.fori_loop(0, tm, start, 0)

    def wait(r, carry):
        for _ in range(TOP_K):
            copy(0, 0).wait()
        return carry

    lax.fori_loop(0, tm, wait, 0)


def _dispatch(vfp, dest, n_rows):
    n, kw = vfp.shape
    tm = DISPATCH_TILE if n % DISPATCH_TILE == 0 else 128
    dest3 = dest.reshape(n // tm, 1, tm * TOP_K)
    xs0 = jnp.zeros((n_rows, kw), U32)
    return pl.pallas_call(
        _dispatch_kernel,
        out_shape=jax.ShapeDtypeStruct((n_rows, kw), U32),
        grid=(n // tm,),
        in_specs=[
            pl.BlockSpec((1, 1, tm * TOP_K), lambda i: (i, 0, 0), memory_space=pltpu.SMEM),
            pl.BlockSpec((tm, kw), lambda i: (i, 0)),
            pl.BlockSpec(memory_space=pl.ANY),
        ],
        out_specs=pl.BlockSpec(memory_space=pl.ANY),
        scratch_shapes=[pltpu.SemaphoreType.DMA(())],
        input_output_aliases={2: 0},
        compiler_params=_cparams("arbitrary"),
        name="moe_dispatch",
    )(dest3, vfp, xs0)


def _ffn_kernel(be_ref, on_ref, x_ref, w1_ref, w3_ref, w2_ref, o_ref, w13b, w2b):
    i = pl.program_id(0)
    prev = be_ref[jnp.maximum(i - 1, 0)]
    ff = w1_ref.shape[2]

    @pl.when((i == 0) | (be_ref[i] != prev))
    def _():
        w13b[:, :ff] = w1_ref[0].astype(BF16)
        w13b[:, ff:] = w3_ref[0].astype(BF16)
        w2b[...] = w2_ref[0].astype(BF16)

    @pl.when(on_ref[i] != 0)
    def _():
        x = _unpack_pair(x_ref[...])
        ac = jnp.dot(x, w13b[...], preferred_element_type=F32)
        a = ac[:, :ff]
        hmid = (a * _sigmoid(a) * ac[:, ff:]).astype(BF16)
        o_ref[...] = _pack_pair(jnp.dot(hmid, w2b[...], preferred_element_type=F32))

    @pl.when(on_ref[i] == 0)
    def _():
        o_ref[...] = jnp.zeros_like(o_ref)


def _grouped_ffn(xs, blk_e, blk_on, w1, w3, w2, tm):
    n_rows, kw = xs.shape
    d, ff = w1.shape[1], w1.shape[2]
    return pl.pallas_call(
        _ffn_kernel,
        out_shape=jax.ShapeDtypeStruct((n_rows, kw), U32),
        grid_spec=pltpu.PrefetchScalarGridSpec(
            num_scalar_prefetch=2,
            grid=(n_rows // tm,),
            in_specs=[
                pl.BlockSpec((tm, kw), lambda i, be, on: (i, 0)),
                pl.BlockSpec((1, d, ff), lambda i, be, on: (be[i], 0, 0)),
                pl.BlockSpec((1, d, ff), lambda i, be, on: (be[i], 0, 0)),
                pl.BlockSpec((1, ff, d), lambda i, be, on: (be[i], 0, 0)),
            ],
            out_specs=pl.BlockSpec((tm, kw), lambda i, be, on: (i, 0)),
            scratch_shapes=[pltpu.VMEM((d, 2 * ff), BF16), pltpu.VMEM((ff, d), BF16)],
        ),
        compiler_params=_cparams("arbitrary"),
        name="moe_ffn",
    )(blk_e, blk_on, xs, w1, w3, w2)


def _combine_kernel(dest_ref, dnext_ref, gate_ref, sh_ref, h_ref, gf_ref, g_ref, b_ref, ys_ref, o_ref, buf_ref, sem):
    tm = h_ref.shape[0]
    i = pl.program_id(0)
    slot = i % 2

    def copy(sl, r, j, d):
        return pltpu.make_async_copy(ys_ref.at[pl.ds(d, 1)], buf_ref.at[sl, j, pl.ds(r, 1)], sem.at[sl])

    def start_tile(idx_ref, sl):
        def start(r, carry):
            for j in range(TOP_K):
                copy(sl, r, j, idx_ref[0, 0, r * TOP_K + j]).start(priority=j % 2)
            return carry

        lax.fori_loop(0, tm, start, 0)

    @pl.when(i == 0)
    def _():
        start_tile(dest_ref, 0)

    @pl.when(i + 1 < pl.num_programs(0))
    def _():
        start_tile(dnext_ref, 1 - slot)

    def wait(r, carry):
        for _ in range(TOP_K):
            copy(slot, 0, 0, 0).wait()
        return carry

    lax.fori_loop(0, tm, wait, 0)

    f = _unpack_pair(sh_ref[...]).astype(F32)
    gate = gate_ref[...]
    for j in range(TOP_K):
        f = f + gate[:, j:j + 1] * _unpack_pair(buf_ref[slot, j]).astype(F32)
    o_ref[...] = _layer_norm(DEEPNORM_ALPHA * h_ref[...] + gf_ref[0] * f, g_ref[...], b_ref[...])


def _combine(ys, dest, gate, shared, h, mod3, ln_g, ln_b, dims):
    n, d = h.shape
    kw = d // 2
    tm = COMBINE_TILE
    seq = _seq_of_tile(dims, tm)
    dest3 = dest.reshape(n // tm, 1, tm * TOP_K)
    return pl.pallas_call(
        _combine_kernel,
        out_shape=jax.ShapeDtypeStruct((n, d), F32),
        grid=(n // tm,),
        in_specs=[
            pl.BlockSpec((1, 1, tm * TOP_K), lambda i: (i, 0, 0), memory_space=pltpu.SMEM),
            pl.BlockSpec((1, 1, tm * TOP_K), lambda i: (jnp.minimum(i + 1, n // tm - 1), 0, 0),
                         memory_space=pltpu.SMEM),
            pl.BlockSpec((tm, LANES), lambda i: (i, 0)),
            pl.BlockSpec((tm, kw), lambda i: (i, 0)),
            pl.BlockSpec((tm, d), lambda i: (i, 0)),
            pl.BlockSpec((1, 1, d), lambda i: (seq(i) * 6 + 5, 0, 0)),
            pl.BlockSpec((1, d), lambda i: (0, 0)),
            pl.BlockSpec((1, d), lambda i: (0, 0)),
            pl.BlockSpec(memory_space=pl.ANY),
        ],
        out_specs=pl.BlockSpec((tm, d), lambda i: (i, 0)),
        scratch_shapes=[pltpu.VMEM((2, TOP_K, tm, kw), U32), pltpu.SemaphoreType.DMA((2,))],
        compiler_params=_cparams("arbitrary"),
        name="moe_combine_ln",
    )(dest3, dest3, gate, shared, h, mod3, ln_g.reshape(1, d), ln_b.reshape(1, d), ys)


def _moe(vfp, h, mod3, router_w, bias, w1, w3, w2, sw1, sw3, sw2, layer, ln_g, ln_b, dims):
    n = vfp.shape[0]
    tmb = MOE_BLOCK
    idx, gate, rank, cnt = _router(vfp, router_w, bias)
    counts = cnt[0, :N_EXPERTS]
    padded = (counts + tmb - 1) // tmb * tmb
    pend = jnp.cumsum(padded)
    pstart = pend - padded
    n_blocks = (n * TOP_K + N_EXPERTS * (tmb - 1) + tmb - 1) // tmb
    blk_pos = jnp.arange(n_blocks, dtype=jnp.int32) * tmb
    blk_e = jnp.minimum(jnp.sum((pend[None, :] <= blk_pos[:, None]).astype(jnp.int32), axis=1), N_EXPERTS - 1)
    blk_on = (blk_pos < pend[-1]).astype(jnp.int32)
    e_sel = idx[:, :TOP_K]
    onehot = e_sel[:, :, None] == jnp.arange(N_EXPERTS, dtype=jnp.int32)[None, None, :]
    dest = rank[:, :TOP_K] + jnp.sum(jnp.where(onehot, pstart[None, None, :], 0), axis=-1)
    dest = dest.reshape(-1).astype(jnp.int32)
    xs = _dispatch(vfp, dest, n_blocks * tmb)
    flat = lambda w: w.reshape((-1,) + w.shape[-2:])
    ys = _grouped_ffn(xs, blk_e + layer * N_EXPERTS, blk_on, flat(w1), flat(w3), flat(w2), tmb)
    ones = jnp.ones((n // tmb,), jnp.int32)
    shared = _grouped_ffn(vfp, jnp.full((n // tmb,), layer, jnp.int32), ones, sw1, sw3, sw2, tmb)
    return _combine(ys, dest, gate, shared, h, mod3, ln_g, ln_b, dims)


def _to_scan(rows, b):
    t = rows.shape[0] // b
    return rows.reshape(b, t, RW_HEADS, RW_HEAD).transpose(1, 3, 0, 2).reshape(t, RW_HEAD, b * RW_HEADS)


def _from_scan(y, b):
    t = y.shape[0]
    return y.reshape(t, RW_HEAD, b, RW_HEADS).transpose(2, 0, 3, 1).reshape(b * t, RW_WIDTH)


def _even_params(j, even_w_in, rw_mu, rw_w0, rw_w_up, rw_a0, rw_a_up, rw_g_up, rw_kk, rw_ka, rw_rk, rw_gn_g,
                 rw_gn_b, cv_w, cv_b, cv_ln_g, cv_ln_b):
    w3 = 3 * RW_WIDTH
    o_g, o_w, o_a = w3, w3 + GATE_LORA, w3 + GATE_LORA + 2 * DECAY_LORA
    o_cv = o_a + 2 * ICLR_LORA

    def relay(m):
        lead = m.shape[0]
        z = lambda k: jnp.zeros((lead, k), m.dtype)
        pieces = [m[:, :w3], m[:, o_g:o_g + GATE_LORA], z(LANES - GATE_LORA)]
        for d in range(2):
            pieces += [m[:, o_w + d * DECAY_LORA:o_w + (d + 1) * DECAY_LORA], z(LANES - DECAY_LORA)]
        for d in range(2):
            pieces += [m[:, o_a + d * ICLR_LORA:o_a + (d + 1) * ICLR_LORA], z(LANES - ICLR_LORA)]
        return jnp.concatenate(pieces, axis=1)

    w_in = even_w_in[j]
    w_pad = jnp.concatenate([relay(w_in), jnp.zeros((w_in.shape[0], 2 * 2048 - RW_STREAM_PAD), w_in.dtype),
                             w_in[:, o_cv:]], axis=1).astype(BF16)
    pad_rows = lambda m, k: jnp.concatenate([m, jnp.zeros((k - m.shape[0], m.shape[1]), m.dtype)], axis=0)
    head = jnp.arange(RW_WIDTH) // RW_HEAD
    vec = lambda m: m.reshape(1, -1)
    return {
        "w_in": w_pad,
        "mu": relay(rw_mu[j]),
        "w0": rw_w0[j], "a0": rw_a0[j],
        "w_up": jnp.stack([pad_rows(rw_w_up[j, d], LANES) for d in range(2)]).astype(BF16),
        "a_up": jnp.stack([pad_rows(rw_a_up[j, d], LANES) for d in range(2)]).astype(BF16),
        "g_up": pad_rows(rw_g_up[j], LANES).astype(BF16),
        "kk": vec(rw_kk[j]), "ka": vec(rw_ka[j]), "rk": vec(rw_rk[j]),
        "hs": (head[:, None] == head[None, :]).astype(BF16),
        "gn_g": vec(rw_gn_g[j]), "gn_b": vec(rw_gn_b[j]),
        "cv_w": pad_rows(cv_w[j], 32), "cv_b": vec(cv_b[j]), "cv_g": vec(cv_ln_g[j]), "cv_beta": vec(cv_ln_b[j]),
    }


def _even_mixer(h, mod3, prm, dims):
    p = _inproj(h, mod3, prm["w_in"], dims, 1, 0)
    r, v, kk, g, bon, dec, bvec, kd = _rwkv_features(p, prm, dims)
    b = dims[0]
    nl = b * dims[1]
    state = [jnp.zeros((RW_HEAD, RW_HEAD, LANES), F32)] * 2
    y_parts = []
    for rows in (slice(nl, None), slice(0, nl)):
        shared = [_to_scan(x[rows], b) for x in (r, kk, v)]
        y_sum = None
        for d in range(2):
            per_dir = [_to_scan(x[d, rows], b) for x in (dec, bvec, kd)]
            y_sum, state[d] = _wkv_scan(*shared, *per_dir, state[d], reverse=(d == 1), y_prev=y_sum)
        y_parts.append(_from_scan(y_sum, b))
    rw = _rwkv_post(jnp.concatenate([y_parts[1], y_parts[0]], axis=0), bon, g, prm)
    cv = _conformer_conv(p, prm, dims)
    return [rw, cv]


def _attn_mixer(h, mod3, w_in, qn, kn, tables, dims, with_ctx):
    p = _inproj(h, mod3, w_in, dims, 1, 0)
    q, k, v = _qkv_post(p, tables[0], tables[1], qn, kn, dims)
    out = _attention(q, k, v, dims, ctx_queries=False)
    if with_ctx:
        out_c = _attention(q, k, v, dims, ctx_queries=True)
    else:
        out_c = jnp.zeros((dims[0] * dims[2], ATT_Q), BF16)
    return [jnp.concatenate([out, out_c], axis=0)]


def kernel(x, c, ctx, c_ctx, ada_w, ada_b, ln1_g, ln1_b, ln2_g, ln2_b, even_w_in, even_w_out, rw_mu, rw_w0, rw_w_up, rw_a0, rw_a_up, rw_g_up, rw_kk, rw_ka, rw_rk, rw_gn_g, rw_gn_b, cv_w, cv_b, cv_ln_g, cv_ln_b, odd_w_in, odd_w_out, q_norm, k_norm, moe_router, moe_bias, moe_w1, moe_w3, moe_w2, sh_w1, sh_w3, sh_w2):
    bsz, s_len, d = x.shape
    ctx_len = ctx.shape[1]
    dims = (bsz, s_len, ctx_len)
    assert bsz * RW_HEADS == LANES, "the WKV scan maps (batch, head) pairs onto the 128 lanes"
    n_lat = bsz * s_len
    depth = ada_w.shape[0]

    c16 = jnp.zeros((16, d), F32).at[:bsz].set(c).at[bsz].set(c_ctx)
    mod_all = _ada_all(c16, ada_w, ada_b)[:, :bsz + 1].reshape(depth, (bsz + 1) * 6, 1, d)
    tables = _rope_tables(s_len, min(ROW_TILE, ctx_len))

    h = jnp.concatenate([x.reshape(n_lat, d), ctx.reshape(bsz * ctx_len, d)], axis=0)
    for layer in range(depth):
        j = layer // 2
        last = layer == depth - 1
        mod3 = mod_all[layer]
        if layer % 2 == 0:
            prm = _even_params(j, even_w_in, rw_mu, rw_w0, rw_w_up, rw_a0, rw_a_up, rw_g_up, rw_kk, rw_ka, rw_rk,
                               rw_gn_g, rw_gn_b, cv_w, cv_b, cv_ln_g, cv_ln_b)
            acts = _even_mixer(h, mod3, prm, dims)
            w_out = even_w_out[j].astype(BF16)
        else:
            acts = _attn_mixer(h, mod3, odd_w_in[j].astype(BF16), q_norm[j], k_norm[j], tables, dims,
                               with_ctx=not last)
            w_out = odd_w_out[j].astype(BF16)
        h, vfp = _outproj(acts, w_out, h, mod3, ln1_g[layer], ln1_b[layer], dims)
        h = _moe(vfp, h, mod3, moe_router[layer], moe_bias[layer], moe_w1, moe_w3, moe_w2, sh_w1, sh_w3, sh_w2,
                 layer, ln2_g[layer], ln2_b[layer], dims)
    return h[:n_lat].reshape(bsz, s_len, d)
```

```python
import functools
import math

import jax
import jax.numpy as jnp
from jax import lax
from jax.experimental import pallas as pl
from jax.experimental.pallas import tpu as pltpu

F32 = jnp.float32
BF16 = jnp.bfloat16
U32 = jnp.uint32

D_MODEL = 2048
DEPTH = 4
GRID_W = 64

RW_HEADS = 16
RW_HEAD = 64
RW_WIDTH = RW_HEADS * RW_HEAD
DECAY_LORA = 96
ICLR_LORA = 96
GATE_LORA = 64
RW_GN_EPS = 64e-5
CONV_WIDTH = 1024
CONV_K = 31

ATT_HEADS = 16
ATT_KV_HEADS = 4
ATT_HEAD = 128
ATT_GROUP = ATT_HEADS // ATT_KV_HEADS
ATT_Q = ATT_HEADS * ATT_HEAD
ATT_KV = ATT_KV_HEADS * ATT_HEAD
ROPE_THETA = 10000.0
ROPE_PAIRS = ATT_HEAD // 4
QK_EPS = 1e-6

N_EXPERTS = 64
TOP_K = 6
EXPERT_FF = 384
ROUTE_SCALE = 2.5

DEEPNORM_ALPHA = (2 * DEPTH) ** 0.25
LN_EPS = 1e-5

LANES = 128
SMALL_W = 640
RW_STREAM_PAD = 3 * RW_WIDTH + SMALL_W
EVEN_IN_PAD = 6144
CONV_COL_BLOCK = 2
EXP_M05 = math.exp(-0.5)
NEG_BIG = -1e30
VMEM_LIMIT = 56 * 1024 * 1024

ROW_TILE = 256
HALO = 16
SCAN_STEPS = 32
MOE_BLOCK = 512
ROUTER_TILE = 512
DISPATCH_TILE = 256
COMBINE_TILE = 128
ATTN_ROW_BLOCK = 128


def _cparams(*sem):
    return pltpu.CompilerParams(dimension_semantics=tuple(sem), vmem_limit_bytes=VMEM_LIMIT)


def _layer_norm(x, g, b):
    mu = jnp.mean(x, axis=-1, keepdims=True)
    xc = x - mu
    var = jnp.mean(xc * xc, axis=-1, keepdims=True)
    return xc * lax.rsqrt(var + LN_EPS) * g + b


def _sigmoid(x):
    return 1.0 / (1.0 + jnp.exp(-x))


def _pack_pair(x):
    k = x.shape[1] // 2
    lo = lax.bitcast_convert_type(x[:, :k].astype(BF16).astype(F32), U32) >> 16
    hi = lax.bitcast_convert_type(x[:, k:].astype(BF16).astype(F32), U32) & jnp.uint32(0xFFFF0000)
    return hi | lo


def _unpack_pair(w):
    lo = lax.bitcast_convert_type(w << 16, F32)
    hi = lax.bitcast_convert_type(w & jnp.uint32(0xFFFF0000), F32)
    return jnp.concatenate([lo, hi], axis=1).astype(BF16)


TOKEN_ROWS = 8


def _load_tokens(ref, n_tok):
    return jnp.concatenate([ref[pl.ds(s, n_tok, stride=TOKEN_ROWS), :] for s in range(TOKEN_ROWS)], axis=1)


def _store_tokens(ref, words):
    n_tok = words.shape[0]
    for s in range(TOKEN_ROWS):
        ref[pl.ds(s, n_tok, stride=TOKEN_ROWS), :] = words[:, s * LANES:(s + 1) * LANES]


def _col_tile(n, cap=1280):
    best = LANES
    for t in range(LANES, cap + 1, LANES):
        if n % t == 0:
            best = t
    return best


def _ada_kernel(c_ref, w_ref, b_ref, o_ref):
    c = c_ref[...]
    s = (c * _sigmoid(c)).astype(BF16)
    o_ref[0] = jnp.dot(s, w_ref[0].astype(BF16), preferred_element_type=F32) + b_ref[0]


def _ada_all(c16, ada_w, ada_b):
    depth, d, n6 = ada_w.shape
    tn = 1024
    return pl.pallas_call(
        _ada_kernel,
        out_shape=jax.ShapeDtypeStruct((depth, 16, n6), F32),
        grid=(depth, n6 // tn),
        in_specs=[
            pl.BlockSpec((16, d), lambda l, j: (0, 0)),
            pl.BlockSpec((1, d, tn), lambda l, j: (l, 0, j)),
            pl.BlockSpec((1, 1, tn), lambda l, j: (l, 0, j)),
        ],
        out_specs=pl.BlockSpec((1, 16, tn), lambda l, j: (l, 0, j)),
        compiler_params=_cparams("parallel", "parallel"),
        name="ada_mod",
    )(c16, ada_w, ada_b.reshape(depth, 1, n6))


def _inproj_kernel(x_ref, sc_ref, sh_ref, w_ref, o_ref, u_ref):
    @pl.when(pl.program_id(1) == 0)
    def _():
        u_ref[...] = (x_ref[...] * (1.0 + sc_ref[0]) + sh_ref[0]).astype(BF16)

    o_ref[...] = jnp.dot(u_ref[...], w_ref[...], preferred_element_type=F32).astype(o_ref.dtype)


def _inproj(h, mod3, w, dims, sc_idx, sh_idx):
    n, d = h.shape
    nout = w.shape[1]
    tm = min(512, dims[1])
    tn = _col_tile(nout)
    seq = _seq_of_tile(dims, tm)
    return pl.pallas_call(
        _inproj_kernel,
        out_shape=jax.ShapeDtypeStruct((n, nout), BF16),
        grid=(n // tm, nout // tn),
        in_specs=[
            pl.BlockSpec((tm, d), lambda i, j: (i, 0)),
            pl.BlockSpec((1, 1, d), lambda i, j: (seq(i) * 6 + sc_idx, 0, 0)),
            pl.BlockSpec((1, 1, d), lambda i, j: (seq(i) * 6 + sh_idx, 0, 0)),
            pl.BlockSpec((d, tn), lambda i, j: (0, j)),
        ],
        out_specs=pl.BlockSpec((tm, tn), lambda i, j: (i, j)),
        scratch_shapes=[pltpu.VMEM((tm, d), BF16)],
        compiler_params=_cparams("parallel", "arbitrary"),
        name="inproj",
    )(h, mod3, mod3, w)


def _seq_of_tile(dims, tm):
    b, s, _ = dims

    def seq(i):
        return jnp.minimum((i * tm) // s, b)

    return seq


def _outproj_kernel(*refs, splits):
    n_in = len(splits)
    a_refs = refs[:n_in]
    w_ref, h_ref, ga_ref, scf_ref, shf_ref, g_ref, b_ref, ho_ref, vf_ref = refs[n_in:]
    acc = None
    off = 0
    for a_ref, k in zip(a_refs, splits):
        part = jnp.dot(a_ref[...], w_ref[off:off + k, :], preferred_element_type=F32)
        acc = part if acc is None else acc + part
        off += k
    hn = _layer_norm(DEEPNORM_ALPHA * h_ref[...] + ga_ref[0] * acc, g_ref[...], b_ref[...])
    ho_ref[...] = hn
    _store_tokens(vf_ref, _pack_pair(hn * (1.0 + scf_ref[0]) + shf_ref[0]))


def _outproj(acts, w, h, mod3, ln_g, ln_b, dims):
    n, d = h.shape
    tm = min(256, dims[1])
    seq = _seq_of_tile(dims, tm)
    splits = tuple(a.shape[1] for a in acts)
    in_specs = [pl.BlockSpec((tm, k), lambda i: (i, 0)) for k in splits]
    in_specs += [
        pl.BlockSpec(w.shape, lambda i: (0, 0)),
        pl.BlockSpec((tm, d), lambda i: (i, 0)),
        pl.BlockSpec((1, 1, d), lambda i: (seq(i) * 6 + 2, 0, 0)),
        pl.BlockSpec((1, 1, d), lambda i: (seq(i) * 6 + 4, 0, 0)),
        pl.BlockSpec((1, 1, d), lambda i: (seq(i) * 6 + 3, 0, 0)),
        pl.BlockSpec((1, d), lambda i: (0, 0)),
        pl.BlockSpec((1, d), lambda i: (0, 0)),
    ]
    return pl.pallas_call(
        functools.partial(_outproj_kernel, splits=splits),
        out_shape=(jax.ShapeDtypeStruct((n, d), F32), jax.ShapeDtypeStruct((n * TOKEN_ROWS, LANES), U32)),
        grid=(n // tm,),
        in_specs=in_specs,
        out_specs=(pl.BlockSpec((tm, d), lambda i: (i, 0)), pl.BlockSpec((tm * TOKEN_ROWS, LANES), lambda i: (i, 0))),
        compiler_params=_cparams("parallel"),
        name="outproj_ln",
    )(*acts, w, h, mod3, mod3, mod3, ln_g.reshape(1, d), ln_b.reshape(1, d))


def _tile_flags(dims, tb):
    b, s, ctx = dims
    n = b * (s + ctx)
    nlt = (b * s) // tb
    nbl = s // tb
    nbc = ctx // tb

    def first_last(i):
        il = i % nbl
        ic = (i - nlt) % nbc
        lat = i < nlt
        first = jnp.where(lat, il == 0, ic == 0)
        last = jnp.where(lat, il == nbl - 1, ic == nbc - 1)
        return first, last

    def prev_blk(i):
        return jnp.maximum((i * tb) // HALO - 1, 0)

    def next_blk(i):
        return jnp.minimum(((i + 1) * tb) // HALO, n // HALO - 1)

    return first_last, prev_blk, next_blk


def _head_sum(x, hs_ref):
    hi = x.astype(BF16)
    lo = (x - hi.astype(F32)).astype(BF16)
    hs = hs_ref[...]
    return jnp.dot(hi, hs, preferred_element_type=F32) + jnp.dot(lo, hs, preferred_element_type=F32)


def _feat_kernel(p_ref, pp_ref, pn_ref, mu_ref, w0_ref, a0_ref, wup_ref, aup_ref, gup_ref, kk_ref, ka_ref,
                 rk_ref, hs_ref, r_o, v_o, kkn_o, g_o, bon_o, dec_o, b_o, kd_o, buf_ref, *, tb, first_last):
    first, last = first_last(pl.program_id(0))
    w = RW_WIDTH
    buf_ref[8:8 + tb, :] = p_ref[...].astype(F32)
    buf_ref[7:8, :] = jnp.where(first, 0.0, pp_ref[HALO - 1:HALO, :].astype(F32))
    buf_ref[8 + tb:9 + tb, :] = jnp.where(last, 0.0, pn_ref[0:1, :].astype(F32))

    def shifted(c0, c1):
        cur = buf_ref[8:8 + tb, c0:c1]
        prev = buf_ref[7:7 + tb, c0:c1]
        nxt = buf_ref[9:9 + tb, c0:c1]
        return cur + mu_ref[0:1, c0:c1] * (prev - cur) + mu_ref[1:2, c0:c1] * (nxt - cur)

    r = shifted(0, w)
    k = shifted(w, 2 * w)
    v = shifted(2 * w, 3 * w)
    sm = shifted(3 * w, 3 * w + SMALL_W)

    g = jnp.dot(_sigmoid(sm[:, 0:LANES]).astype(BF16), gup_ref[...], preferred_element_type=F32)
    kkr = k * kk_ref[...]
    kk = kkr * lax.rsqrt(_head_sum(kkr * kkr, hs_ref) + 1e-12)

    ksum = None
    for d in range(2):
        wd = sm[:, LANES * (1 + d):LANES * (2 + d)]
        ad = sm[:, LANES * (3 + d):LANES * (4 + d)]
        xw = w0_ref[d:d + 1, :] + jnp.dot(jnp.tanh(wd).astype(BF16), wup_ref[d], preferred_element_type=F32)
        dec_o[d] = jnp.exp(-EXP_M05 * _sigmoid(xw))
        a = _sigmoid(a0_ref[d:d + 1, :] + jnp.dot(ad.astype(BF16), aup_ref[d], preferred_element_type=F32))
        b_o[d] = (a * kk).astype(BF16)
        kd = k * (1.0 + (a - 1.0) * ka_ref[...])
        kd_o[d] = kd.astype(BF16)
        ksum = kd if ksum is None else ksum + kd

    bon_o[...] = (_head_sum(r * ksum * rk_ref[...], hs_ref) * v).astype(BF16)
    r_o[...] = r.astype(BF16)
    v_o[...] = v.astype(BF16)
    kkn_o[...] = kk.astype(BF16)
    g_o[...] = g.astype(BF16)


def _rwkv_features(p, prm, dims):
    n = p.shape[0]
    tb = min(ROW_TILE, dims[2])
    first_last, prev_blk, next_blk = _tile_flags(dims, tb)
    w = RW_WIDTH
    pw = RW_STREAM_PAD
    full2 = lambda shape: pl.BlockSpec(shape, lambda i: (0, 0))
    full3 = lambda shape: pl.BlockSpec(shape, lambda i: (0, 0, 0))
    row = pl.BlockSpec((tb, w), lambda i: (i, 0))
    row2 = pl.BlockSpec((2, tb, w), lambda i: (0, i, 0))
    sd = lambda dt: jax.ShapeDtypeStruct((n, w), dt)
    sd2 = lambda dt: jax.ShapeDtypeStruct((2, n, w), dt)
    return pl.pallas_call(
        functools.partial(_feat_kernel, tb=tb, first_last=first_last),
        out_shape=(sd(BF16), sd(BF16), sd(BF16), sd(BF16), sd(BF16), sd2(F32), sd2(BF16), sd2(BF16)),
        grid=(n // tb,),
        in_specs=[
            pl.BlockSpec((tb, pw), lambda i: (i, 0)),
            pl.BlockSpec((HALO, pw), lambda i: (prev_blk(i), 0)),
            pl.BlockSpec((HALO, pw), lambda i: (next_blk(i), 0)),
            full2((2, pw)), full2((2, w)), full2((2, w)),
            full3((2, LANES, w)), full3((2, LANES, w)), full2((LANES, w)),
            full2((1, w)), full2((1, w)), full2((1, w)), full2((w, w)),
        ],
        out_specs=(row, row, row, row, row, row2, row2, row2),
        scratch_shapes=[pltpu.VMEM((tb + 16, pw), F32)],
        compiler_params=_cparams("parallel"),
        name="rwkv_features",
    )(p, p, p, prm["mu"], prm["w0"], prm["a0"], prm["w_up"], prm["a_up"], prm["g_up"], prm["kk"], prm["ka"],
      prm["rk"], prm["hs"])


def _wkv_kernel(r_ref, kk_ref, v_ref, w_ref, b_ref, kd_ref, s0_ref, *refs, steps, reverse, add_prev):
    yp_ref = refs[0] if add_prev else None
    y_ref, st_ref, s_ref, rf, kkf, vf, bf, kdf = refs[1:] if add_prev else refs
    i = pl.program_id(0)

    @pl.when(i == 0)
    def _():
        s_ref[...] = s0_ref[...]

    for src, dst in ((r_ref, rf), (kk_ref, kkf), (v_ref, vf), (b_ref, bf), (kd_ref, kdf)):
        dst[...] = src[...].astype(F32)

    nk = RW_HEAD
    half = nk // 2
    t_first = steps - 1 if reverse else 0

    def s_dot_kk(t, lo):
        acc = jnp.zeros((half, LANES), F32)
        for k in range(nk):
            acc = acc + s_ref[k, lo:lo + half, :] * kkf[t, pl.ds(k, 1), :]
        return acc

    def step(j, sks):
        t = steps - 1 - j if reverse else j
        t_next = jnp.maximum(t - 1, 0) if reverse else jnp.minimum(t + 1, steps - 1)
        sks_next = []
        for sk, lo in zip(sks, (0, half)):
            vt = vf[t, lo:lo + half, :]
            y = jnp.zeros((half, LANES), F32)
            sk_next = jnp.zeros((half, LANES), F32)
            for k in range(nk):
                s = (s_ref[k, lo:lo + half, :] * w_ref[t, pl.ds(k, 1), :] - sk * bf[t, pl.ds(k, 1), :]
                     + vt * kdf[t, pl.ds(k, 1), :])
                s_ref[k, lo:lo + half, :] = s
                y = y + s * rf[t, pl.ds(k, 1), :]
                sk_next = sk_next + s * kkf[t_next, pl.ds(k, 1), :]
            y_ref[t, lo:lo + half, :] = y + yp_ref[t, lo:lo + half, :] if add_prev else y
            sks_next.append(sk_next)
        return tuple(sks_next)

    lax.fori_loop(0, steps, step, (s_dot_kk(t_first, 0), s_dot_kk(t_first, half)))

    @pl.when(i == pl.num_programs(0) - 1)
    def _():
        st_ref[...] = s_ref[...]


def _wkv_scan(r, kk, v, w, b, kd, s0, reverse, y_prev=None):
    t_all = r.shape[0]
    extra = [] if y_prev is None else [y_prev]
    steps = min(SCAN_STEPS, t_all)
    nblk = t_all // steps
    blk = (lambda i: (nblk - 1 - i, 0, 0)) if reverse else (lambda i: (i, 0, 0))
    seq = pl.BlockSpec((steps, RW_HEAD, LANES), blk)
    state = pl.BlockSpec((RW_HEAD, RW_HEAD, LANES), lambda i: (0, 0, 0))
    buf = pltpu.VMEM((steps, RW_HEAD, LANES), F32)
    return pl.pallas_call(
        functools.partial(_wkv_kernel, steps=steps, reverse=reverse, add_prev=y_prev is not None),
        out_shape=(jax.ShapeDtypeStruct((t_all, RW_HEAD, LANES), F32),
                   jax.ShapeDtypeStruct((RW_HEAD, RW_HEAD, LANES), F32)),
        grid=(nblk,),
        in_specs=[seq, seq, seq, seq, seq, seq, state] + [seq] * len(extra),
        out_specs=(seq, state),
        scratch_shapes=[pltpu.VMEM((RW_HEAD, RW_HEAD, LANES), F32), buf, buf, buf, buf, buf],
        compiler_params=_cparams("arbitrary"),
        name="wkv_scan_bwd" if reverse else "wkv_scan_fwd",
    )(r, kk, v, w, b, kd, s0, *extra)


def _rwpost_kernel(y_ref, bon_ref, g_ref, gng_ref, gnb_ref, hs_ref, o_ref):
    y = y_ref[...]
    inv = 1.0 / RW_HEAD
    yc = y - _head_sum(y, hs_ref) * inv
    var = _head_sum(yc * yc, hs_ref) * inv
    yn = yc * lax.rsqrt(var + RW_GN_EPS) * gng_ref[...] + gnb_ref[...]
    o_ref[...] = ((yn + bon_ref[...].astype(F32)) * g_ref[...].astype(F32)).astype(BF16)


def _rwkv_post(y, bon, g, prm):
    n, w = y.shape
    tb = 256
    row = pl.BlockSpec((tb, w), lambda i: (i, 0))
    vec = pl.BlockSpec((1, w), lambda i: (0, 0))
    return pl.pallas_call(
        _rwpost_kernel,
        out_shape=jax.ShapeDtypeStruct((n, w), BF16),
        grid=(n // tb,),
        in_specs=[row, row, row, vec, vec, pl.BlockSpec((w, w), lambda i: (0, 0))],
        out_specs=row,
        compiler_params=_cparams("parallel"),
        name="rwkv_post",
    )(y, bon, g, prm["gn_g"], prm["gn_b"], prm["hs"])


def _conv_kernel(p_ref, pp_ref, pn_ref, w_ref, b_ref, g_ref, beta_ref, o_ref, z_ref, *, tb, first_last):
    first, last = first_last(pl.program_id(0))
    c = CONV_WIDTH

    def glu(x):
        x = x.astype(F32)
        return x[:, :c] * _sigmoid(x[:, c:])

    z_ref[HALO:HALO + tb, :] = glu(p_ref[...])
    z_ref[0:HALO, :] = jnp.where(first, 0.0, glu(pp_ref[...]))
    z_ref[HALO + tb:2 * HALO + tb, :] = jnp.where(last, 0.0, glu(pn_ref[...]))
    acc = jnp.zeros((tb, c), F32)
    base = HALO - CONV_K // 2
    for j in range(CONV_K):
        acc = acc + z_ref[base + j:base + j + tb, :] * w_ref[j:j + 1, :]
    zn = _layer_norm(acc + b_ref[...], g_ref[...], beta_ref[...])
    o_ref[...] = (zn * _sigmoid(zn)).astype(BF16)


def _conformer_conv(p, prm, dims):
    n = p.shape[0]
    tb = min(ROW_TILE, dims[2])
    first_last, prev_blk, next_blk = _tile_flags(dims, tb)
    c = CONV_WIDTH
    cb = CONV_COL_BLOCK
    vec = pl.BlockSpec((1, c), lambda i: (0, 0))
    return pl.pallas_call(
        functools.partial(_conv_kernel, tb=tb, first_last=first_last),
        out_shape=jax.ShapeDtypeStruct((n, c), BF16),
        grid=(n // tb,),
        in_specs=[
            pl.BlockSpec((tb, 2 * c), lambda i: (i, cb)),
            pl.BlockSpec((HALO, 2 * c), lambda i: (prev_blk(i), cb)),
            pl.BlockSpec((HALO, 2 * c), lambda i: (next_blk(i), cb)),
            pl.BlockSpec((32, c), lambda i: (0, 0)), vec, vec, vec,
        ],
        out_specs=pl.BlockSpec((tb, c), lambda i: (i, 0)),
        scratch_shapes=[pltpu.VMEM((tb + 2 * HALO, c), F32)],
        compiler_params=_cparams("parallel"),
        name="conformer_conv",
    )(p, p, p, prm["cv_w"], prm["cv_b"], prm["cv_g"], prm["cv_beta"])


def _qkv_kernel(p_ref, cos_ref, sin_ref, qn_ref, kn_ref, q_o, k_o, v_o):
    lane = lax.broadcasted_iota(jnp.int32, (1, ATT_HEAD), 1)
    first_half = (lane % 64) < 32
    cosv = cos_ref[...]
    sinv = sin_ref[...]

    def norm_rope(x, gain, scale):
        x = x.astype(F32)
        xn = x * lax.rsqrt(jnp.mean(x * x, axis=-1, keepdims=True) + QK_EPS) * gain
        swapped = jnp.where(first_half, pltpu.roll(xn, ATT_HEAD - 32, axis=1), pltpu.roll(xn, 32, axis=1))
        return (xn * cosv + swapped * sinv) * scale

    hd = ATT_HEAD
    ones = jnp.ones((p_ref.shape[0], hd), BF16)
    for h in range(ATT_HEADS):
        q_o[h] = norm_rope(p_ref[:, h * hd:(h + 1) * hd], qn_ref[...], 1.0 / math.sqrt(ATT_HEAD)).astype(BF16)
    for h in range(ATT_KV_HEADS):
        k_o[h] = norm_rope(p_ref[:, ATT_Q + h * hd:ATT_Q + (h + 1) * hd], kn_ref[...], 1.0).T.astype(BF16)
        v_o[h, :, :hd] = p_ref[:, ATT_Q + ATT_KV + h * hd:ATT_Q + ATT_KV + (h + 1) * hd]
        v_o[h, :, hd:] = ones


def _qkv_post(p, cos_t, sin_t, qn, kn, dims):
    b, s, ctx = dims
    n = p.shape[0]
    tb = min(ROW_TILE, ctx)
    nlt = (b * s) // tb
    nbl = s // tb
    hd = ATT_HEAD

    def tab(i):
        return jnp.where(i < nlt, i % nbl, nbl)

    return pl.pallas_call(
        _qkv_kernel,
        out_shape=(jax.ShapeDtypeStruct((ATT_HEADS, n, hd), BF16),
                   jax.ShapeDtypeStruct((ATT_KV_HEADS, hd, n), BF16),
                   jax.ShapeDtypeStruct((ATT_KV_HEADS, n, 2 * hd), BF16)),
        grid=(n // tb,),
        in_specs=[
            pl.BlockSpec((tb, p.shape[1]), lambda i: (i, 0)),
            pl.BlockSpec((tb, hd), lambda i: (tab(i), 0)),
            pl.BlockSpec((tb, hd), lambda i: (tab(i), 0)),
            pl.BlockSpec((1, hd), lambda i: (0, 0)),
            pl.BlockSpec((1, hd), lambda i: (0, 0)),
        ],
        out_specs=(pl.BlockSpec((ATT_HEADS, tb, hd), lambda i: (0, i, 0)),
                   pl.BlockSpec((ATT_KV_HEADS, hd, tb), lambda i: (0, 0, i)),
                   pl.BlockSpec((ATT_KV_HEADS, tb, 2 * hd), lambda i: (0, i, 0))),
        compiler_params=_cparams("parallel"),
        name="qkv_norm_rope",
    )(p, cos_t, sin_t, qn.reshape(1, hd), kn.reshape(1, hd))


def _rope_tables(s, pad_rows):
    rows_n = s // GRID_W
    row = jnp.repeat(jnp.arange(rows_n), GRID_W)
    col = jnp.tile(jnp.arange(GRID_W), rows_n)
    inv = ROPE_THETA ** (-jnp.arange(ROPE_PAIRS, dtype=F32) / ROPE_PAIRS)
    ang_r = row[:, None] * inv
    ang_c = col[:, None] * inv
    cos_t = jnp.concatenate([jnp.cos(ang_r), jnp.cos(ang_r), jnp.cos(ang_c), jnp.cos(ang_c)], axis=1)
    sin_t = jnp.concatenate([-jnp.sin(ang_r), jnp.sin(ang_r), -jnp.sin(ang_c), jnp.sin(ang_c)], axis=1)
    cos_t = jnp.concatenate([cos_t, jnp.ones((pad_rows, ATT_HEAD), F32)], axis=0)
    sin_t = jnp.concatenate([sin_t, jnp.zeros((pad_rows, ATT_HEAD), F32)], axis=0)
    return cos_t, sin_t


def _attn_kernel(q_ref, *refs, seg_lens, kc):
    n_seg = len(seg_lens)
    o_ref, s_ref, mx_ref, p_ref = refs[2 * n_seg:]
    g, tq, hd = q_ref.shape

    @pl.when(pl.program_id(0) == 0)
    def _():
        p_ref[...] = jnp.zeros_like(p_ref)

    acc = None
    off = 0
    for si, n_keys in enumerate(seg_lens):
        pv = jnp.dot(p_ref[:, off:off + n_keys], refs[2 * si + 1][0], preferred_element_type=F32)
        acc = pv if acc is None else acc + pv
        off += n_keys

    q = q_ref[...].reshape(g * tq, hd)
    chunks = []
    off = 0
    for si, n_keys in enumerate(seg_lens):
        for c0 in range(0, n_keys, kc):
            c = min(kc, n_keys - c0)
            chunks.append((si, c0, c, off))
            off += c
    for n, (si, c0, c, off) in enumerate(chunks):
        s = jnp.dot(q, refs[2 * si][0, :, c0:c0 + c], preferred_element_type=F32)
        s_ref[:, off:off + c] = s
        part = s[:, :LANES]
        for j in range(1, c // LANES):
            part = jnp.maximum(part, s[:, j * LANES:(j + 1) * LANES])
        mx_ref[...] = part if n == 0 else jnp.maximum(mx_ref[...], part)

    o = acc[:, :hd] / jnp.maximum(acc[:, hd:], 1e-30)
    for gi in range(g):
        o_ref[:, gi * hd:(gi + 1) * hd] = o[gi * tq:(gi + 1) * tq, :].astype(o_ref.dtype)

    mx_ref[...] = jnp.broadcast_to(jnp.max(mx_ref[...], axis=-1, keepdims=True), mx_ref.shape)
    rb = min(ATTN_ROW_BLOCK, g * tq)

    def exp_rows(i, carry):
        r0 = pl.multiple_of(i * rb, rb)
        mb = mx_ref[pl.ds(r0, rb), :]
        for _, _, c, off in chunks:
            e = jnp.exp(s_ref[pl.ds(r0, rb), off:off + c] - jnp.concatenate([mb] * (c // LANES), axis=1))
            p_ref[pl.ds(r0, rb), off:off + c] = e.astype(BF16)
        return carry

    lax.fori_loop(0, (g * tq) // rb, exp_rows, 0)


def _attention(q, kt, v1, dims, ctx_queries):
    b, s, ctx = dims
    hd = ATT_HEAD
    g = ATT_GROUP
    kvh = ATT_KV_HEADS
    nl = b * s
    if ctx_queries:
        tq, nq, rows, q_base = ctx, 1, b * ctx, nl // ctx
        segs = [(ctx, nl // ctx)]
    else:
        tq = min(256, s)
        nq, rows, q_base = s // tq, nl, 0
        segs = [(s, 0), (ctx, nl // ctx)]
    n_tiles = b * kvh * nq

    def split(tile):
        return tile // (kvh * nq), (tile // nq) % kvh, tile % nq

    cur = lambda t: split(jnp.minimum(t, n_tiles - 1))
    prev = lambda t: split(jnp.maximum(t - 1, 0))

    def q_map(t):
        bi, kv, qi = cur(t)
        return kv, q_base + bi * nq + qi, 0

    def o_map(t):
        bi, kv, qi = prev(t)
        return bi * nq + qi, kv

    in_specs = [pl.BlockSpec((g, tq, hd), q_map)]
    args = [q]
    for n_keys, base in segs:
        def k_map(t, base=base):
            bi, kv, _ = cur(t)
            return kv, 0, base + bi

        def v_map(t, base=base):
            bi, kv, _ = prev(t)
            return kv, base + bi, 0

        in_specs += [pl.BlockSpec((1, hd, n_keys), k_map), pl.BlockSpec((1, n_keys, 2 * hd), v_map)]
        args += [kt, v1]
    n_all = sum(n for n, _ in segs)
    return pl.pallas_call(
        functools.partial(_attn_kernel, seg_lens=tuple(n for n, _ in segs), kc=512),
        out_shape=jax.ShapeDtypeStruct((rows, ATT_Q), BF16),
        grid=(n_tiles + 1,),
        in_specs=in_specs,
        out_specs=pl.BlockSpec((tq, g * hd), o_map),
        scratch_shapes=[pltpu.VMEM((g * tq, n_all), F32), pltpu.VMEM((g * tq, LANES), F32),
                        pltpu.VMEM((g * tq, n_all), BF16)],
        compiler_params=_cparams("arbitrary"),
        name="gqa_attention_ctx" if ctx_queries else "gqa_attention",
    )(*args)


def _router_kernel(x_ref, w_ref, bias_ref, tri_ref, idx_o, gate_o, rank_o, cnt_o, carry_ref):
    @pl.when(pl.program_id(0) == 0)
    def _():
        carry_ref[...] = jnp.zeros_like(carry_ref)

    tm = x_ref.shape[0] // TOKEN_ROWS
    x = _unpack_pair(_load_tokens(x_ref, tm))
    scores = _sigmoid(jnp.dot(x, w_ref[...], preferred_element_type=F32))
    lane = lax.broadcasted_iota(jnp.int32, (tm, LANES), 1)
    sel = jnp.where(lane < N_EXPERTS, scores + bias_ref[...], NEG_BIG)
    picks = []
    gates = []
    chosen = jnp.zeros((tm, LANES), F32)
    for _ in range(TOP_K):
        mx = jnp.max(sel, axis=-1, keepdims=True)
        idx = jnp.min(jnp.where(sel == mx, lane, LANES), axis=-1, keepdims=True)
        hit = lane == idx
        picks.append((idx, hit))
        gates.append(jnp.sum(jnp.where(hit, scores, 0.0), axis=-1, keepdims=True))
        chosen = chosen + hit.astype(F32)
        sel = jnp.where(hit, NEG_BIG, sel)
    gsum = gates[0]
    for gj in gates[1:]:
        gsum = gsum + gj
    before = jnp.dot(tri_ref[...], chosen.astype(BF16), preferred_element_type=F32) + carry_ref[...]
    idx_out = jnp.zeros((tm, LANES), jnp.int32)
    gate_out = jnp.zeros((tm, LANES), F32)
    rank_out = jnp.zeros((tm, LANES), jnp.int32)
    for j, (idx, hit) in enumerate(picks):
        rank = jnp.sum(jnp.where(hit, before, 0.0), axis=-1, keepdims=True).astype(jnp.int32)
        idx_out = jnp.where(lane == j, idx, idx_out)
        gate_out = jnp.where(lane == j, ROUTE_SCALE * gates[j] / gsum, gate_out)
        rank_out = jnp.where(lane == j, rank, rank_out)
    idx_o[...] = idx_out
    gate_o[...] = gate_out
    rank_o[...] = rank_out
    carry_ref[...] = carry_ref[...] + jnp.sum(chosen, axis=0, keepdims=True)
    cnt_o[...] = carry_ref[...].astype(jnp.int32)


def _router(vfp, router_w, bias):
    n = vfp.shape[0] // TOKEN_ROWS
    tm = ROUTER_TILE if n % ROUTER_TILE == 0 else 128
    d = router_w.shape[0]
    wpad = jnp.zeros((d, LANES), BF16).at[:, :N_EXPERTS].set(router_w.astype(BF16))
    bpad = jnp.zeros((1, LANES), F32).at[0, :N_EXPERTS].set(bias)
    tri = (lax.broadcasted_iota(jnp.int32, (tm, tm), 1) < lax.broadcasted_iota(jnp.int32, (tm, tm), 0)).astype(BF16)
    row = pl.BlockSpec((tm, LANES), lambda i: (i, 0))
    return pl.pallas_call(
        _router_kernel,
        out_shape=(jax.ShapeDtypeStruct((n, LANES), jnp.int32), jax.ShapeDtypeStruct((n, LANES), F32),
                   jax.ShapeDtypeStruct((n, LANES), jnp.int32), jax.ShapeDtypeStruct((1, LANES), jnp.int32)),
        grid=(n // tm,),
        in_specs=[
            pl.BlockSpec((tm * TOKEN_ROWS, LANES), lambda i: (i, 0)),
            pl.BlockSpec((d, LANES), lambda i: (0, 0)),
            pl.BlockSpec((1, LANES), lambda i: (0, 0)),
            pl.BlockSpec((tm, tm), lambda i: (0, 0)),
        ],
        out_specs=(row, row, row, pl.BlockSpec((1, LANES), lambda i: (0, 0))),
        scratch_shapes=[pltpu.VMEM((1, LANES), F32)],
        compiler_params=_cparams("arbitrary"),
        name="moe_router",
    )(vfp, wpad, bpad, tri)


def _token_tile(ref, t):
    return ref.at[pl.ds(pl.multiple_of(t * TOKEN_ROWS, TOKEN_ROWS), TOKEN_ROWS)]


def _dispatch_kernel(dest_ref, x_ref, xs_in_ref, xs_ref, sem):
    del xs_in_ref
    tm = x_ref.shape[0] // TOKEN_ROWS

    def copy(r, d):
        return pltpu.make_async_copy(_token_tile(x_ref, r), _token_tile(xs_ref, d), sem)

    def start(r, carry):
        for j in range(TOP_K):
            copy(r, dest_ref[0, 0, r * TOP_K + j]).start(priority=j % 2)
        return carry

    lax.fori_loop(0, tm, start, 0, unroll=2)

    def wait(r, carry):
        for _ in range(TOP_K):
            copy(0, 0).wait()
        return carry

    lax.fori_loop(0, tm, wait, 0)


def _dispatch(vfp, dest, n_rows):
    n = vfp.shape[0] // TOKEN_ROWS
    tm = DISPATCH_TILE if n % DISPATCH_TILE == 0 else 128
    dest3 = dest.reshape(n // tm, 1, tm * TOP_K)
    xs0 = jnp.zeros((n_rows * TOKEN_ROWS, LANES), U32)
    return pl.pallas_call(
        _dispatch_kernel,
        out_shape=jax.ShapeDtypeStruct((n_rows * TOKEN_ROWS, LANES), U32),
        grid=(n // tm,),
        in_specs=[
            pl.BlockSpec((1, 1, tm * TOP_K), lambda i: (i, 0, 0), memory_space=pltpu.SMEM),
            pl.BlockSpec((tm * TOKEN_ROWS, LANES), lambda i: (i, 0)),
            pl.BlockSpec(memory_space=pl.ANY),
        ],
        out_specs=pl.BlockSpec(memory_space=pl.ANY),
        scratch_shapes=[pltpu.SemaphoreType.DMA(())],
        input_output_aliases={2: 0},
        compiler_params=_cparams("arbitrary"),
        name="moe_dispatch",
    )(dest3, vfp, xs0)


def _ffn_kernel(be_ref, on_ref, x_ref, w1_ref, w3_ref, w2_ref, o_ref, w13b, w2b):
    i = pl.program_id(0)
    prev = be_ref[jnp.maximum(i - 1, 0)]
    ff = w1_ref.shape[2]

    @pl.when((i == 0) | (be_ref[i] != prev))
    def _():
        w13b[:, :ff] = w1_ref[0].astype(BF16)
        w13b[:, ff:] = w3_ref[0].astype(BF16)
        w2b[...] = w2_ref[0].astype(BF16)

    @pl.when(on_ref[i] != 0)
    def _():
        x = _unpack_pair(_load_tokens(x_ref, x_ref.shape[0] // TOKEN_ROWS))
        ac = jnp.dot(x, w13b[...], preferred_element_type=F32)
        a = ac[:, :ff]
        hmid = (a * _sigmoid(a) * ac[:, ff:]).astype(BF16)
        _store_tokens(o_ref, _pack_pair(jnp.dot(hmid, w2b[...], preferred_element_type=F32)))

    @pl.when(on_ref[i] == 0)
    def _():
        o_ref[...] = jnp.zeros_like(o_ref)


def _grouped_ffn(xs, blk_e, blk_on, w1, w3, w2, tm):
    n_rows = xs.shape[0] // TOKEN_ROWS
    d, ff = w1.shape[1], w1.shape[2]
    tile = pl.BlockSpec((tm * TOKEN_ROWS, LANES), lambda i, be, on: (i, 0))
    return pl.pallas_call(
        _ffn_kernel,
        out_shape=jax.ShapeDtypeStruct(xs.shape, U32),
        grid_spec=pltpu.PrefetchScalarGridSpec(
            num_scalar_prefetch=2,
            grid=(n_rows // tm,),
            in_specs=[
                tile,
                pl.BlockSpec((1, d, ff), lambda i, be, on: (be[i], 0, 0)),
                pl.BlockSpec((1, d, ff), lambda i, be, on: (be[i], 0, 0)),
                pl.BlockSpec((1, ff, d), lambda i, be, on: (be[i], 0, 0)),
            ],
            out_specs=tile,
            scratch_shapes=[pltpu.VMEM((d, 2 * ff), BF16), pltpu.VMEM((ff, d), BF16)],
        ),
        compiler_params=_cparams("arbitrary"),
        name="moe_ffn",
    )(blk_e, blk_on, xs, w1, w3, w2)


def _combine_kernel(dest_ref, dnext_ref, gate_ref, sh_ref, h_ref, gf_ref, g_ref, b_ref, ys_ref, o_ref, buf_ref, sem):
    tm = h_ref.shape[0]
    i = pl.program_id(0)
    slot = i % 2

    def copy(sl, r, j, d):
        return pltpu.make_async_copy(_token_tile(ys_ref, d), _token_tile(buf_ref.at[sl, j], r), sem.at[sl])

    def start_tile(idx_ref, sl):
        def start(r, carry):
            for j in range(TOP_K):
                copy(sl, r, j, idx_ref[0, 0, r * TOP_K + j]).start(priority=j % 2)
            return carry

        lax.fori_loop(0, tm, start, 0, unroll=2)

    @pl.when(i == 0)
    def _():
        start_tile(dest_ref, 0)

    @pl.when(i + 1 < pl.num_programs(0))
    def _():
        start_tile(dnext_ref, 1 - slot)

    def wait(r, carry):
        for _ in range(TOP_K):
            copy(slot, 0, 0, 0).wait()
        return carry

    lax.fori_loop(0, tm, wait, 0)

    f = _unpack_pair(_load_tokens(sh_ref, tm)).astype(F32)
    gate = gate_ref[...]
    for j in range(TOP_K):
        f = f + gate[:, j:j + 1] * _unpack_pair(_load_tokens(buf_ref.at[slot, j], tm)).astype(F32)
    o_ref[...] = _layer_norm(DEEPNORM_ALPHA * h_ref[...] + gf_ref[0] * f, g_ref[...], b_ref[...])


def _combine(ys, dest, gate, shared, h, mod3, ln_g, ln_b, dims):
    n, d = h.shape
    tm = COMBINE_TILE
    seq = _seq_of_tile(dims, tm)
    dest3 = dest.reshape(n // tm, 1, tm * TOP_K)
    return pl.pallas_call(
        _combine_kernel,
        out_shape=jax.ShapeDtypeStruct((n, d), F32),
        grid=(n // tm,),
        in_specs=[
            pl.BlockSpec((1, 1, tm * TOP_K), lambda i: (i, 0, 0), memory_space=pltpu.SMEM),
            pl.BlockSpec((1, 1, tm * TOP_K), lambda i: (jnp.minimum(i + 1, n // tm - 1), 0, 0),
                         memory_space=pltpu.SMEM),
            pl.BlockSpec((tm, LANES), lambda i: (i, 0)),
            pl.BlockSpec((tm * TOKEN_ROWS, LANES), lambda i: (i, 0)),
            pl.BlockSpec((tm, d), lambda i: (i, 0)),
            pl.BlockSpec((1, 1, d), lambda i: (seq(i) * 6 + 5, 0, 0)),
            pl.BlockSpec((1, d), lambda i: (0, 0)),
            pl.BlockSpec((1, d), lambda i: (0, 0)),
            pl.BlockSpec(memory_space=pl.ANY),
        ],
        out_specs=pl.BlockSpec((tm, d), lambda i: (i, 0)),
        scratch_shapes=[pltpu.VMEM((2, TOP_K, tm * TOKEN_ROWS, LANES), U32), pltpu.SemaphoreType.DMA((2,))],
        compiler_params=_cparams("arbitrary"),
        name="moe_combine_ln",
    )(dest3, dest3, gate, shared, h, mod3, ln_g.reshape(1, d), ln_b.reshape(1, d), ys)


def _moe(vfp, h, mod3, router_w, bias, w1, w3, w2, sw1, sw3, sw2, layer, ln_g, ln_b, dims):
    n = vfp.shape[0] // TOKEN_ROWS
    tmb = MOE_BLOCK
    idx, gate, rank, cnt = _router(vfp, router_w, bias)
    counts = cnt[0, :N_EXPERTS]
    padded = (counts + tmb - 1) // tmb * tmb
    pend = jnp.cumsum(padded)
    pstart = pend - padded
    n_blocks = (n * TOP_K + N_EXPERTS * (tmb - 1) + tmb - 1) // tmb
    blk_pos = jnp.arange(n_blocks, dtype=jnp.int32) * tmb
    blk_e = jnp.minimum(jnp.sum((pend[None, :] <= blk_pos[:, None]).astype(jnp.int32), axis=1), N_EXPERTS - 1)
    blk_on = (blk_pos < pend[-1]).astype(jnp.int32)
    e_sel = idx[:, :TOP_K]
    onehot = e_sel[:, :, None] == jnp.arange(N_EXPERTS, dtype=jnp.int32)[None, None, :]
    dest = rank[:, :TOP_K] + jnp.sum(jnp.where(onehot, pstart[None, None, :], 0), axis=-1)
    dest = dest.reshape(-1).astype(jnp.int32)
    xs = _dispatch(vfp, dest, n_blocks * tmb)
    flat = lambda w: w.reshape((-1,) + w.shape[-2:])
    ys = _grouped_ffn(xs, blk_e + layer * N_EXPERTS, blk_on, flat(w1), flat(w3), flat(w2), tmb)
    ones = jnp.ones((n // tmb,), jnp.int32)
    shared = _grouped_ffn(vfp, jnp.full((n // tmb,), layer, jnp.int32), ones, sw1, sw3, sw2, tmb)
    return _combine(ys, dest, gate, shared, h, mod3, ln_g, ln_b, dims)


def _to_scan(rows, b):
    t = rows.shape[0] // b
    return rows.reshape(b, t, RW_HEADS, RW_HEAD).transpose(1, 3, 0, 2).reshape(t, RW_HEAD, b * RW_HEADS)


def _from_scan(y, b):
    t = y.shape[0]
    return y.reshape(t, RW_HEAD, b, RW_HEADS).transpose(2, 0, 3, 1).reshape(b * t, RW_WIDTH)


def _even_params(j, even_w_in, rw_mu, rw_w0, rw_w_up, rw_a0, rw_a_up, rw_g_up, rw_kk, rw_ka, rw_rk, rw_gn_g,
                 rw_gn_b, cv_w, cv_b, cv_ln_g, cv_ln_b):
    w3 = 3 * RW_WIDTH
    o_g, o_w, o_a = w3, w3 + GATE_LORA, w3 + GATE_LORA + 2 * DECAY_LORA
    o_cv = o_a + 2 * ICLR_LORA

    def relay(m):
        lead = m.shape[0]
        z = lambda k: jnp.zeros((lead, k), m.dtype)
        pieces = [m[:, :w3], m[:, o_g:o_g + GATE_LORA], z(LANES - GATE_LORA)]
        for d in range(2):
            pieces += [m[:, o_w + d * DECAY_LORA:o_w + (d + 1) * DECAY_LORA], z(LANES - DECAY_LORA)]
        for d in range(2):
            pieces += [m[:, o_a + d * ICLR_LORA:o_a + (d + 1) * ICLR_LORA], z(LANES - ICLR_LORA)]
        return jnp.concatenate(pieces, axis=1)

    w_in = even_w_in[j]
    w_pad = jnp.concatenate([relay(w_in), jnp.zeros((w_in.shape[0], 2 * 2048 - RW_STREAM_PAD), w_in.dtype),
                             w_in[:, o_cv:]], axis=1).astype(BF16)
    pad_rows = lambda m, k: jnp.concatenate([m, jnp.zeros((k - m.shape[0], m.shape[1]), m.dtype)], axis=0)
    head = jnp.arange(RW_WIDTH) // RW_HEAD
    vec = lambda m: m.reshape(1, -1)
    return {
        "w_in": w_pad,
        "mu": relay(rw_mu[j]),
        "w0": rw_w0[j], "a0": rw_a0[j],
        "w_up": jnp.stack([pad_rows(rw_w_up[j, d], LANES) for d in range(2)]).astype(BF16),
        "a_up": jnp.stack([pad_rows(rw_a_up[j, d], LANES) for d in range(2)]).astype(BF16),
        "g_up": pad_rows(rw_g_up[j], LANES).astype(BF16),
        "kk": vec(rw_kk[j]), "ka": vec(rw_ka[j]), "rk": vec(rw_rk[j]),
        "hs": (head[:, None] == head[None, :]).astype(BF16),
        "gn_g": vec(rw_gn_g[j]), "gn_b": vec(rw_gn_b[j]),
        "cv_w": pad_rows(cv_w[j], 32), "cv_b": vec(cv_b[j]), "cv_g": vec(cv_ln_g[j]), "cv_beta": vec(cv_ln_b[j]),
    }


def _even_mixer(h, mod3, prm, dims):
    p = _inproj(h, mod3, prm["w_in"], dims, 1, 0)
    r, v, kk, g, bon, dec, bvec, kd = _rwkv_features(p, prm, dims)
    b = dims[0]
    nl = b * dims[1]
    state = [jnp.zeros((RW_HEAD, RW_HEAD, LANES), F32)] * 2
    y_parts = []
    for rows in (slice(nl, None), slice(0, nl)):
        shared = [_to_scan(x[rows], b) for x in (r, kk, v)]
        y_sum = None
        for d in range(2):
            per_dir = [_to_scan(x[d, rows], b) for x in (dec, bvec, kd)]
            y_sum, state[d] = _wkv_scan(*shared, *per_dir, state[d], reverse=(d == 1), y_prev=y_sum)
        y_parts.append(_from_scan(y_sum, b))
    rw = _rwkv_post(jnp.concatenate([y_parts[1], y_parts[0]], axis=0), bon, g, prm)
    cv = _conformer_conv(p, prm, dims)
    return [rw, cv]


def _attn_mixer(h, mod3, w_in, qn, kn, tables, dims, with_ctx):
    p = _inproj(h, mod3, w_in, dims, 1, 0)
    q, k, v = _qkv_post(p, tables[0], tables[1], qn, kn, dims)
    out = _attention(q, k, v, dims, ctx_queries=False)
    if with_ctx:
        out_c = _attention(q, k, v, dims, ctx_queries=True)
    else:
        out_c = jnp.zeros((dims[0] * dims[2], ATT_Q), BF16)
    return [jnp.concatenate([out, out_c], axis=0)]


def kernel(x, c, ctx, c_ctx, ada_w, ada_b, ln1_g, ln1_b, ln2_g, ln2_b, even_w_in, even_w_out, rw_mu, rw_w0, rw_w_up, rw_a0, rw_a_up, rw_g_up, rw_kk, rw_ka, rw_rk, rw_gn_g, rw_gn_b, cv_w, cv_b, cv_ln_g, cv_ln_b, odd_w_in, odd_w_out, q_norm, k_norm, moe_router, moe_bias, moe_w1, moe_w3, moe_w2, sh_w1, sh_w3, sh_w2):
    bsz, s_len, d = x.shape
    ctx_len = ctx.shape[1]
    dims = (bsz, s_len, ctx_len)
    assert bsz * RW_HEADS == LANES, "the WKV scan maps (batch, head) pairs onto the 128 lanes"
    n_lat = bsz * s_len
    depth = ada_w.shape[0]

    c16 = jnp.zeros((16, d), F32).at[:bsz].set(c).at[bsz].set(c_ctx)
    mod_all = _ada_all(c16, ada_w, ada_b)[:, :bsz + 1].reshape(depth, (bsz + 1) * 6, 1, d)
    tables = _rope_tables(s_len, min(ROW_TILE, ctx_len))

    h = jnp.concatenate([x.reshape(n_lat, d), ctx.reshape(bsz * ctx_len, d)], axis=0)
    for layer in range(depth):
        j = layer // 2
        last = layer == depth - 1
        mod3 = mod_all[layer]
        if layer % 2 == 0:
            prm = _even_params(j, even_w_in, rw_mu, rw_w0, rw_w_up, rw_a0, rw_a_up, rw_g_up, rw_kk, rw_ka, rw_rk,
                               rw_gn_g, rw_gn_b, cv_w, cv_b, cv_ln_g, cv_ln_b)
            acts = _even_mixer(h, mod3, prm, dims)
            w_out = even_w_out[j].astype(BF16)
        else:
            acts = _attn_mixer(h, mod3, odd_w_in[j].astype(BF16), q_norm[j], k_norm[j], tables, dims,
                               with_ctx=not last)
            w_out = odd_w_out[j].astype(BF16)
        h, vfp = _outproj(acts, w_out, h, mod3, ln1_g[layer], ln1_b[layer], dims)
        h = _moe(vfp, h, mod3, moe_router[layer], moe_bias[layer], moe_w1, moe_w3, moe_w2, sh_w1, sh_w3, sh_w2,
                 layer, ln2_g[layer], ln2_b[layer], dims)
    return h[:n_lat].reshape(bsz, s_len, d)
```

```python
import functools
import math

import jax
import jax.numpy as jnp
from jax import lax
from jax.experimental import pallas as pl
from jax.experimental.pallas import tpu as pltpu

F32 = jnp.float32
BF16 = jnp.bfloat16
U32 = jnp.uint32

D_MODEL = 2048
DEPTH = 4
GRID_W = 64

RW_HEADS = 16
RW_HEAD = 64
RW_WIDTH = RW_HEADS * RW_HEAD
DECAY_LORA = 96
ICLR_LORA = 96
GATE_LORA = 64
RW_GN_EPS = 64e-5
CONV_WIDTH = 1024
CONV_K = 31

ATT_HEADS = 16
ATT_KV_HEADS = 4
ATT_HEAD = 128
ATT_GROUP = ATT_HEADS // ATT_KV_HEADS
ATT_Q = ATT_HEADS * ATT_HEAD
ATT_KV = ATT_KV_HEADS * ATT_HEAD
ROPE_THETA = 10000.0
ROPE_PAIRS = ATT_HEAD // 4
QK_EPS = 1e-6

N_EXPERTS = 64
TOP_K = 6
EXPERT_FF = 384
ROUTE_SCALE = 2.5

DEEPNORM_ALPHA = (2 * DEPTH) ** 0.25
LN_EPS = 1e-5

LANES = 128
SMALL_W = 640
RW_STREAM_PAD = 3 * RW_WIDTH + SMALL_W
EVEN_IN_PAD = 6144
CONV_COL_BLOCK = 2
EXP_M05 = math.exp(-0.5)
NEG_BIG = -1e30
VMEM_LIMIT = 56 * 1024 * 1024

ROW_TILE = 256
HALO = 16
SCAN_STEPS = 64
MOE_BLOCK = 512
ROUTER_TILE = 512
DISPATCH_TILE = 256
COMBINE_TILE = 128
ATTN_ROW_BLOCK = 128


def _cparams(*sem):
    return pltpu.CompilerParams(dimension_semantics=tuple(sem), vmem_limit_bytes=VMEM_LIMIT)


def _layer_norm(x, g, b):
    mu = jnp.mean(x, axis=-1, keepdims=True)
    xc = x - mu
    var = jnp.mean(xc * xc, axis=-1, keepdims=True)
    return xc * lax.rsqrt(var + LN_EPS) * g + b


def _sigmoid(x):
    return 1.0 / (1.0 + jnp.exp(-x))


def _pack_pair(x):
    k = x.shape[1] // 2
    lo = lax.bitcast_convert_type(x[:, :k].astype(BF16).astype(F32), U32) >> 16
    hi = lax.bitcast_convert_type(x[:, k:].astype(BF16).astype(F32), U32) & jnp.uint32(0xFFFF0000)
    return hi | lo


def _unpack_pair(w):
    lo = lax.bitcast_convert_type(w << 16, F32)
    hi = lax.bitcast_convert_type(w & jnp.uint32(0xFFFF0000), F32)
    return jnp.concatenate([lo, hi], axis=1).astype(BF16)


TOKEN_ROWS = 8


def _load_tokens(ref, n_tok):
    return jnp.concatenate([ref[pl.ds(s, n_tok, stride=TOKEN_ROWS), :] for s in range(TOKEN_ROWS)], axis=1)


def _store_tokens(ref, words):
    n_tok = words.shape[0]
    for s in range(TOKEN_ROWS):
        ref[pl.ds(s, n_tok, stride=TOKEN_ROWS), :] = words[:, s * LANES:(s + 1) * LANES]


def _col_tile(n, cap=1280):
    best = LANES
    for t in range(LANES, cap + 1, LANES):
        if n % t == 0:
            best = t
    return best


def _ada_kernel(c_ref, w_ref, b_ref, o_ref):
    c = c_ref[...]
    s = (c * _sigmoid(c)).astype(BF16)
    o_ref[0] = jnp.dot(s, w_ref[0].astype(BF16), preferred_element_type=F32) + b_ref[0]


def _ada_all(c16, ada_w, ada_b):
    depth, d, n6 = ada_w.shape
    tn = 1024
    return pl.pallas_call(
        _ada_kernel,
        out_shape=jax.ShapeDtypeStruct((depth, 16, n6), F32),
        grid=(depth, n6 // tn),
        in_specs=[
            pl.BlockSpec((16, d), lambda l, j: (0, 0)),
            pl.BlockSpec((1, d, tn), lambda l, j: (l, 0, j)),
            pl.BlockSpec((1, 1, tn), lambda l, j: (l, 0, j)),
        ],
        out_specs=pl.BlockSpec((1, 16, tn), lambda l, j: (l, 0, j)),
        compiler_params=_cparams("parallel", "parallel"),
        name="ada_mod",
    )(c16, ada_w, ada_b.reshape(depth, 1, n6))


def _inproj_kernel(x_ref, sc_ref, sh_ref, w_ref, o_ref, u_ref):
    @pl.when(pl.program_id(1) == 0)
    def _():
        u_ref[...] = (x_ref[...] * (1.0 + sc_ref[0]) + sh_ref[0]).astype(BF16)

    o_ref[...] = jnp.dot(u_ref[...], w_ref[...], preferred_element_type=F32).astype(o_ref.dtype)


def _inproj(h, mod3, w, dims, sc_idx, sh_idx):
    n, d = h.shape
    nout = w.shape[1]
    tm = min(512, dims[1])
    tn = _col_tile(nout)
    seq = _seq_of_tile(dims, tm)
    return pl.pallas_call(
        _inproj_kernel,
        out_shape=jax.ShapeDtypeStruct((n, nout), BF16),
        grid=(n // tm, nout // tn),
        in_specs=[
            pl.BlockSpec((tm, d), lambda i, j: (i, 0)),
            pl.BlockSpec((1, 1, d), lambda i, j: (seq(i) * 6 + sc_idx, 0, 0)),
            pl.BlockSpec((1, 1, d), lambda i, j: (seq(i) * 6 + sh_idx, 0, 0)),
            pl.BlockSpec((d, tn), lambda i, j: (0, j)),
        ],
        out_specs=pl.BlockSpec((tm, tn), lambda i, j: (i, j)),
        scratch_shapes=[pltpu.VMEM((tm, d), BF16)],
        compiler_params=_cparams("parallel", "arbitrary"),
        name="inproj",
    )(h, mod3, mod3, w)


def _seq_of_tile(dims, tm):
    b, s, _ = dims

    def seq(i):
        return jnp.minimum((i * tm) // s, b)

    return seq


def _outproj_kernel(*refs, splits):
    n_in = len(splits)
    a_refs = refs[:n_in]
    w_ref, h_ref, ga_ref, scf_ref, shf_ref, g_ref, b_ref, ho_ref, vf_ref = refs[n_in:]
    acc = None
    off = 0
    for a_ref, k in zip(a_refs, splits):
        part = jnp.dot(a_ref[...], w_ref[off:off + k, :], preferred_element_type=F32)
        acc = part if acc is None else acc + part
        off += k
    hn = _layer_norm(DEEPNORM_ALPHA * h_ref[...] + ga_ref[0] * acc, g_ref[...], b_ref[...])
    ho_ref[...] = hn
    _store_tokens(vf_ref, _pack_pair(hn * (1.0 + scf_ref[0]) + shf_ref[0]))


def _outproj(acts, w, h, mod3, ln_g, ln_b, dims):
    n, d = acts[0].shape[0], h.shape[1]
    tm = min(256, dims[1])
    seq = _seq_of_tile(dims, tm)
    splits = tuple(a.shape[1] for a in acts)
    in_specs = [pl.BlockSpec((tm, k), lambda i: (i, 0)) for k in splits]
    in_specs += [
        pl.BlockSpec(w.shape, lambda i: (0, 0)),
        pl.BlockSpec((tm, d), lambda i: (i, 0)),
        pl.BlockSpec((1, 1, d), lambda i: (seq(i) * 6 + 2, 0, 0)),
        pl.BlockSpec((1, 1, d), lambda i: (seq(i) * 6 + 4, 0, 0)),
        pl.BlockSpec((1, 1, d), lambda i: (seq(i) * 6 + 3, 0, 0)),
        pl.BlockSpec((1, d), lambda i: (0, 0)),
        pl.BlockSpec((1, d), lambda i: (0, 0)),
    ]
    return pl.pallas_call(
        functools.partial(_outproj_kernel, splits=splits),
        out_shape=(jax.ShapeDtypeStruct((n, d), F32), jax.ShapeDtypeStruct((n * TOKEN_ROWS, LANES), U32)),
        grid=(n // tm,),
        in_specs=in_specs,
        out_specs=(pl.BlockSpec((tm, d), lambda i: (i, 0)), pl.BlockSpec((tm * TOKEN_ROWS, LANES), lambda i: (i, 0))),
        compiler_params=_cparams("parallel"),
        name="outproj_ln",
    )(*acts, w, h, mod3, mod3, mod3, ln_g.reshape(1, d), ln_b.reshape(1, d))


def _tile_flags(dims, tb):
    b, s, ctx = dims
    n = b * (s + ctx)
    nlt = (b * s) // tb
    nbl = s // tb
    nbc = ctx // tb

    def first_last(i):
        il = i % nbl
        ic = (i - nlt) % nbc
        lat = i < nlt
        first = jnp.where(lat, il == 0, ic == 0)
        last = jnp.where(lat, il == nbl - 1, ic == nbc - 1)
        return first, last

    def prev_blk(i):
        return jnp.maximum((i * tb) // HALO - 1, 0)

    def next_blk(i):
        return jnp.minimum(((i + 1) * tb) // HALO, n // HALO - 1)

    return first_last, prev_blk, next_blk


def _head_sum(x, hs_ref):
    hi = x.astype(BF16)
    lo = (x - hi.astype(F32)).astype(BF16)
    hs = hs_ref[...]
    return jnp.dot(hi, hs, preferred_element_type=F32) + jnp.dot(lo, hs, preferred_element_type=F32)


def _feat_kernel(p_ref, pp_ref, pn_ref, mu_ref, w0_ref, a0_ref, wup_ref, aup_ref, gup_ref, kk_ref, ka_ref,
                 rk_ref, hs_ref, r_o, v_o, kkn_o, g_o, bon_o, dec_o, b_o, kd_o, buf_ref, *, tb, first_last):
    first, last = first_last(pl.program_id(0))
    w = RW_WIDTH
    buf_ref[8:8 + tb, :] = p_ref[...].astype(F32)
    buf_ref[7:8, :] = jnp.where(first, 0.0, pp_ref[HALO - 1:HALO, :].astype(F32))
    buf_ref[8 + tb:9 + tb, :] = jnp.where(last, 0.0, pn_ref[0:1, :].astype(F32))

    def shifted(c0, c1):
        cur = buf_ref[8:8 + tb, c0:c1]
        prev = buf_ref[7:7 + tb, c0:c1]
        nxt = buf_ref[9:9 + tb, c0:c1]
        return cur + mu_ref[0:1, c0:c1] * (prev - cur) + mu_ref[1:2, c0:c1] * (nxt - cur)

    r = shifted(0, w)
    k = shifted(w, 2 * w)
    v = shifted(2 * w, 3 * w)
    sm = shifted(3 * w, 3 * w + SMALL_W)

    g = jnp.dot(_sigmoid(sm[:, 0:LANES]).astype(BF16), gup_ref[...], preferred_element_type=F32)
    kkr = k * kk_ref[...]
    kk = kkr * lax.rsqrt(_head_sum(kkr * kkr, hs_ref) + 1e-12)

    ksum = None
    for d in range(2):
        wd = sm[:, LANES * (1 + d):LANES * (2 + d)]
        ad = sm[:, LANES * (3 + d):LANES * (4 + d)]
        xw = w0_ref[d:d + 1, :] + jnp.dot(jnp.tanh(wd).astype(BF16), wup_ref[d], preferred_element_type=F32)
        dec_o[d] = jnp.exp(-EXP_M05 * _sigmoid(xw))
        a = _sigmoid(a0_ref[d:d + 1, :] + jnp.dot(ad.astype(BF16), aup_ref[d], preferred_element_type=F32))
        b_o[d] = (a * kk).astype(BF16)
        kd = k * (1.0 + (a - 1.0) * ka_ref[...])
        kd_o[d] = kd.astype(BF16)
        ksum = kd if ksum is None else ksum + kd

    bon_o[...] = (_head_sum(r * ksum * rk_ref[...], hs_ref) * v).astype(BF16)
    r_o[...] = r.astype(BF16)
    v_o[...] = v.astype(BF16)
    kkn_o[...] = kk.astype(BF16)
    g_o[...] = g.astype(BF16)


def _rwkv_features(p, prm, dims):
    n = p.shape[0]
    tb = min(ROW_TILE, dims[2])
    first_last, prev_blk, next_blk = _tile_flags(dims, tb)
    w = RW_WIDTH
    pw = RW_STREAM_PAD
    full2 = lambda shape: pl.BlockSpec(shape, lambda i: (0, 0))
    full3 = lambda shape: pl.BlockSpec(shape, lambda i: (0, 0, 0))
    row = pl.BlockSpec((tb, w), lambda i: (i, 0))
    row2 = pl.BlockSpec((2, tb, w), lambda i: (0, i, 0))
    sd = lambda dt: jax.ShapeDtypeStruct((n, w), dt)
    sd2 = lambda dt: jax.ShapeDtypeStruct((2, n, w), dt)
    return pl.pallas_call(
        functools.partial(_feat_kernel, tb=tb, first_last=first_last),
        out_shape=(sd(BF16), sd(BF16), sd(BF16), sd(BF16), sd(BF16), sd2(F32), sd2(BF16), sd2(BF16)),
        grid=(n // tb,),
        in_specs=[
            pl.BlockSpec((tb, pw), lambda i: (i, 0)),
            pl.BlockSpec((HALO, pw), lambda i: (prev_blk(i), 0)),
            pl.BlockSpec((HALO, pw), lambda i: (next_blk(i), 0)),
            full2((2, pw)), full2((2, w)), full2((2, w)),
            full3((2, LANES, w)), full3((2, LANES, w)), full2((LANES, w)),
            full2((1, w)), full2((1, w)), full2((1, w)), full2((w, w)),
        ],
        out_specs=(row, row, row, row, row, row2, row2, row2),
        scratch_shapes=[pltpu.VMEM((tb + 16, pw), F32)],
        compiler_params=_cparams("parallel"),
        name="rwkv_features",
    )(p, p, p, prm["mu"], prm["w0"], prm["a0"], prm["w_up"], prm["a_up"], prm["g_up"], prm["kk"], prm["ka"],
      prm["rk"], prm["hs"])


def _wkv_kernel(r_ref, kk_ref, v_ref, w_ref, b_ref, kd_ref, s0_ref, *refs, steps, reverse, add_prev):
    yp_ref = refs[0] if add_prev else None
    y_ref, st_ref, s_ref, rf, kkf, vf, bf, kdf = refs[1:] if add_prev else refs
    i = pl.program_id(0)

    @pl.when(i == 0)
    def _():
        s_ref[...] = s0_ref[...]

    for src, dst in ((r_ref, rf), (kk_ref, kkf), (v_ref, vf), (b_ref, bf), (kd_ref, kdf)):
        dst[...] = src[...].astype(F32)

    nk = RW_HEAD
    half = nk // 2
    t_first = steps - 1 if reverse else 0

    def s_dot_kk(t, lo):
        acc = jnp.zeros((half, LANES), F32)
        for k in range(nk):
            acc = acc + s_ref[k, lo:lo + half, :] * kkf[t, pl.ds(k, 1), :]
        return acc

    def step(j, sks):
        t = steps - 1 - j if reverse else j
        t_next = jnp.maximum(t - 1, 0) if reverse else jnp.minimum(t + 1, steps - 1)
        sks_next = []
        for sk, lo in zip(sks, (0, half)):
            vt = vf[t, lo:lo + half, :]
            y = jnp.zeros((half, LANES), F32)
            sk_next = jnp.zeros((half, LANES), F32)
            for k in range(nk):
                s = (s_ref[k, lo:lo + half, :] * w_ref[t, pl.ds(k, 1), :] - sk * bf[t, pl.ds(k, 1), :]
                     + vt * kdf[t, pl.ds(k, 1), :])
                s_ref[k, lo:lo + half, :] = s
                y = y + s * rf[t, pl.ds(k, 1), :]
                sk_next = sk_next + s * kkf[t_next, pl.ds(k, 1), :]
            y_ref[t, lo:lo + half, :] = y + yp_ref[t, lo:lo + half, :] if add_prev else y
            sks_next.append(sk_next)
        return tuple(sks_next)

    lax.fori_loop(0, steps, step, (s_dot_kk(t_first, 0), s_dot_kk(t_first, half)))

    @pl.when(i == pl.num_programs(0) - 1)
    def _():
        st_ref[...] = s_ref[...]


def _wkv_scan(r, kk, v, w, b, kd, s0, reverse, y_prev=None):
    t_all = r.shape[0]
    extra = [] if y_prev is None else [y_prev]
    steps = min(SCAN_STEPS, t_all)
    nblk = t_all // steps
    blk = (lambda i: (nblk - 1 - i, 0, 0)) if reverse else (lambda i: (i, 0, 0))
    seq = pl.BlockSpec((steps, RW_HEAD, LANES), blk)
    state = pl.BlockSpec((RW_HEAD, RW_HEAD, LANES), lambda i: (0, 0, 0))
    buf = pltpu.VMEM((steps, RW_HEAD, LANES), F32)
    return pl.pallas_call(
        functools.partial(_wkv_kernel, steps=steps, reverse=reverse, add_prev=y_prev is not None),
        out_shape=(jax.ShapeDtypeStruct((t_all, RW_HEAD, LANES), F32),
                   jax.ShapeDtypeStruct((RW_HEAD, RW_HEAD, LANES), F32)),
        grid=(nblk,),
        in_specs=[seq, seq, seq, seq, seq, seq, state] + [seq] * len(extra),
        out_specs=(seq, state),
        scratch_shapes=[pltpu.VMEM((RW_HEAD, RW_HEAD, LANES), F32), buf, buf, buf, buf, buf],
        compiler_params=_cparams("arbitrary"),
        name="wkv_scan_bwd" if reverse else "wkv_scan_fwd",
    )(r, kk, v, w, b, kd, s0, *extra)


def _rwpost_kernel(y_ref, bon_ref, g_ref, gng_ref, gnb_ref, hs_ref, o_ref):
    y = y_ref[...]
    inv = 1.0 / RW_HEAD
    yc = y - _head_sum(y, hs_ref) * inv
    var = _head_sum(yc * yc, hs_ref) * inv
    yn = yc * lax.rsqrt(var + RW_GN_EPS) * gng_ref[...] + gnb_ref[...]
    o_ref[...] = ((yn + bon_ref[...].astype(F32)) * g_ref[...].astype(F32)).astype(BF16)


def _rwkv_post(y, bon, g, prm):
    n, w = y.shape
    tb = 256
    row = pl.BlockSpec((tb, w), lambda i: (i, 0))
    vec = pl.BlockSpec((1, w), lambda i: (0, 0))
    return pl.pallas_call(
        _rwpost_kernel,
        out_shape=jax.ShapeDtypeStruct((n, w), BF16),
        grid=(n // tb,),
        in_specs=[row, row, row, vec, vec, pl.BlockSpec((w, w), lambda i: (0, 0))],
        out_specs=row,
        compiler_params=_cparams("parallel"),
        name="rwkv_post",
    )(y, bon, g, prm["gn_g"], prm["gn_b"], prm["hs"])


def _conv_kernel(p_ref, pp_ref, pn_ref, w_ref, b_ref, g_ref, beta_ref, o_ref, z_ref, *, tb, first_last):
    first, last = first_last(pl.program_id(0))
    c = CONV_WIDTH

    def glu(x):
        x = x.astype(F32)
        return x[:, :c] * _sigmoid(x[:, c:])

    z_ref[HALO:HALO + tb, :] = glu(p_ref[...])
    z_ref[0:HALO, :] = jnp.where(first, 0.0, glu(pp_ref[...]))
    z_ref[HALO + tb:2 * HALO + tb, :] = jnp.where(last, 0.0, glu(pn_ref[...]))
    acc = jnp.zeros((tb, c), F32)
    base = HALO - CONV_K // 2
    for j in range(CONV_K):
        acc = acc + z_ref[base + j:base + j + tb, :] * w_ref[j:j + 1, :]
    zn = _layer_norm(acc + b_ref[...], g_ref[...], beta_ref[...])
    o_ref[...] = (zn * _sigmoid(zn)).astype(BF16)


def _conformer_conv(p, prm, dims):
    n = p.shape[0]
    tb = min(ROW_TILE, dims[2])
    first_last, prev_blk, next_blk = _tile_flags(dims, tb)
    c = CONV_WIDTH
    cb = CONV_COL_BLOCK
    vec = pl.BlockSpec((1, c), lambda i: (0, 0))
    return pl.pallas_call(
        functools.partial(_conv_kernel, tb=tb, first_last=first_last),
        out_shape=jax.ShapeDtypeStruct((n, c), BF16),
        grid=(n // tb,),
        in_specs=[
            pl.BlockSpec((tb, 2 * c), lambda i: (i, cb)),
            pl.BlockSpec((HALO, 2 * c), lambda i: (prev_blk(i), cb)),
            pl.BlockSpec((HALO, 2 * c), lambda i: (next_blk(i), cb)),
            pl.BlockSpec((32, c), lambda i: (0, 0)), vec, vec, vec,
        ],
        out_specs=pl.BlockSpec((tb, c), lambda i: (i, 0)),
        scratch_shapes=[pltpu.VMEM((tb + 2 * HALO, c), F32)],
        compiler_params=_cparams("parallel"),
        name="conformer_conv",
    )(p, p, p, prm["cv_w"], prm["cv_b"], prm["cv_g"], prm["cv_beta"])


def _qkv_kernel(p_ref, cos_ref, sin_ref, qn_ref, kn_ref, q_o, k_o, v_o):
    lane = lax.broadcasted_iota(jnp.int32, (1, ATT_HEAD), 1)
    first_half = (lane % 64) < 32
    cosv = cos_ref[...]
    sinv = sin_ref[...]

    def norm_rope(x, gain, scale):
        x = x.astype(F32)
        xn = x * lax.rsqrt(jnp.mean(x * x, axis=-1, keepdims=True) + QK_EPS) * gain
        swapped = jnp.where(first_half, pltpu.roll(xn, ATT_HEAD - 32, axis=1), pltpu.roll(xn, 32, axis=1))
        return (xn * cosv + swapped * sinv) * scale

    hd = ATT_HEAD
    ones = jnp.ones((p_ref.shape[0], hd), BF16)
    for h in range(ATT_HEADS):
        q_o[h] = norm_rope(p_ref[:, h * hd:(h + 1) * hd], qn_ref[...], 1.0 / math.sqrt(ATT_HEAD)).astype(BF16)
    for h in range(ATT_KV_HEADS):
        k_o[h] = norm_rope(p_ref[:, ATT_Q + h * hd:ATT_Q + (h + 1) * hd], kn_ref[...], 1.0).T.astype(BF16)
        v_o[h, :, :hd] = p_ref[:, ATT_Q + ATT_KV + h * hd:ATT_Q + ATT_KV + (h + 1) * hd]
        v_o[h, :, hd:] = ones


def _qkv_post(p, cos_t, sin_t, qn, kn, dims):
    b, s, ctx = dims
    n = p.shape[0]
    tb = min(ROW_TILE, ctx)
    nlt = (b * s) // tb
    nbl = s // tb
    hd = ATT_HEAD

    def tab(i):
        return jnp.where(i < nlt, i % nbl, nbl)

    return pl.pallas_call(
        _qkv_kernel,
        out_shape=(jax.ShapeDtypeStruct((ATT_HEADS, n, hd), BF16),
                   jax.ShapeDtypeStruct((ATT_KV_HEADS, hd, n), BF16),
                   jax.ShapeDtypeStruct((ATT_KV_HEADS, n, 2 * hd), BF16)),
        grid=(n // tb,),
        in_specs=[
            pl.BlockSpec((tb, p.shape[1]), lambda i: (i, 0)),
            pl.BlockSpec((tb, hd), lambda i: (tab(i), 0)),
            pl.BlockSpec((tb, hd), lambda i: (tab(i), 0)),
            pl.BlockSpec((1, hd), lambda i: (0, 0)),
            pl.BlockSpec((1, hd), lambda i: (0, 0)),
        ],
        out_specs=(pl.BlockSpec((ATT_HEADS, tb, hd), lambda i: (0, i, 0)),
                   pl.BlockSpec((ATT_KV_HEADS, hd, tb), lambda i: (0, 0, i)),
                   pl.BlockSpec((ATT_KV_HEADS, tb, 2 * hd), lambda i: (0, i, 0))),
        compiler_params=_cparams("parallel"),
        name="qkv_norm_rope",
    )(p, cos_t, sin_t, qn.reshape(1, hd), kn.reshape(1, hd))


def _rope_tables(s, pad_rows):
    rows_n = s // GRID_W
    row = jnp.repeat(jnp.arange(rows_n), GRID_W)
    col = jnp.tile(jnp.arange(GRID_W), rows_n)
    inv = ROPE_THETA ** (-jnp.arange(ROPE_PAIRS, dtype=F32) / ROPE_PAIRS)
    ang_r = row[:, None] * inv
    ang_c = col[:, None] * inv
    cos_t = jnp.concatenate([jnp.cos(ang_r), jnp.cos(ang_r), jnp.cos(ang_c), jnp.cos(ang_c)], axis=1)
    sin_t = jnp.concatenate([-jnp.sin(ang_r), jnp.sin(ang_r), -jnp.sin(ang_c), jnp.sin(ang_c)], axis=1)
    cos_t = jnp.concatenate([cos_t, jnp.ones((pad_rows, ATT_HEAD), F32)], axis=0)
    sin_t = jnp.concatenate([sin_t, jnp.zeros((pad_rows, ATT_HEAD), F32)], axis=0)
    return cos_t, sin_t


def _attn_kernel(q_ref, *refs, seg_lens, kc):
    n_seg = len(seg_lens)
    o_ref, s_ref, mx_ref, p_ref = refs[2 * n_seg:]
    g, tq, hd = q_ref.shape

    @pl.when(pl.program_id(0) == 0)
    def _():
        p_ref[...] = jnp.zeros_like(p_ref)

    acc = None
    off = 0
    for si, n_keys in enumerate(seg_lens):
        pv = jnp.dot(p_ref[:, off:off + n_keys], refs[2 * si + 1][0], preferred_element_type=F32)
        acc = pv if acc is None else acc + pv
        off += n_keys

    q = q_ref[...].reshape(g * tq, hd)
    chunks = []
    off = 0
    for si, n_keys in enumerate(seg_lens):
        for c0 in range(0, n_keys, kc):
            c = min(kc, n_keys - c0)
            chunks.append((si, c0, c, off))
            off += c
    for n, (si, c0, c, off) in enumerate(chunks):
        s = jnp.dot(q, refs[2 * si][0, :, c0:c0 + c], preferred_element_type=F32)
        s_ref[:, off:off + c] = s
        part = s[:, :LANES]
        for j in range(1, c // LANES):
            part = jnp.maximum(part, s[:, j * LANES:(j + 1) * LANES])
        mx_ref[...] = part if n == 0 else jnp.maximum(mx_ref[...], part)

    o = acc[:, :hd] / jnp.maximum(acc[:, hd:], 1e-30)
    for gi in range(g):
        o_ref[:, gi * hd:(gi + 1) * hd] = o[gi * tq:(gi + 1) * tq, :].astype(o_ref.dtype)

    mx_ref[...] = jnp.broadcast_to(jnp.max(mx_ref[...], axis=-1, keepdims=True), mx_ref.shape)
    rb = min(ATTN_ROW_BLOCK, g * tq)

    def exp_rows(i, carry):
        r0 = pl.multiple_of(i * rb, rb)
        mb = mx_ref[pl.ds(r0, rb), :]
        for _, _, c, off in chunks:
            e = jnp.exp(s_ref[pl.ds(r0, rb), off:off + c] - jnp.concatenate([mb] * (c // LANES), axis=1))
            p_ref[pl.ds(r0, rb), off:off + c] = e.astype(BF16)
        return carry

    lax.fori_loop(0, (g * tq) // rb, exp_rows, 0)


def _attention(q, kt, v1, dims, ctx_queries):
    b, s, ctx = dims
    hd = ATT_HEAD
    g = ATT_GROUP
    kvh = ATT_KV_HEADS
    nl = b * s
    if ctx_queries:
        tq, nq, rows, q_base = ctx, 1, b * ctx, nl // ctx
        segs = [(ctx, nl // ctx)]
    else:
        tq = min(256, s)
        nq, rows, q_base = s // tq, nl, 0
        segs = [(s, 0), (ctx, nl // ctx)]
    n_tiles = b * kvh * nq

    def split(tile):
        return tile // (kvh * nq), (tile // nq) % kvh, tile % nq

    cur = lambda t: split(jnp.minimum(t, n_tiles - 1))
    prev = lambda t: split(jnp.maximum(t - 1, 0))

    def q_map(t):
        bi, kv, qi = cur(t)
        return kv, q_base + bi * nq + qi, 0

    def o_map(t):
        bi, kv, qi = prev(t)
        return bi * nq + qi, kv

    in_specs = [pl.BlockSpec((g, tq, hd), q_map)]
    args = [q]
    for n_keys, base in segs:
        def k_map(t, base=base):
            bi, kv, _ = cur(t)
            return kv, 0, base + bi

        def v_map(t, base=base):
            bi, kv, _ = prev(t)
            return kv, base + bi, 0

        in_specs += [pl.BlockSpec((1, hd, n_keys), k_map), pl.BlockSpec((1, n_keys, 2 * hd), v_map)]
        args += [kt, v1]
    n_all = sum(n for n, _ in segs)
    return pl.pallas_call(
        functools.partial(_attn_kernel, seg_lens=tuple(n for n, _ in segs), kc=512),
        out_shape=jax.ShapeDtypeStruct((rows, ATT_Q), BF16),
        grid=(n_tiles + 1,),
        in_specs=in_specs,
        out_specs=pl.BlockSpec((tq, g * hd), o_map),
        scratch_shapes=[pltpu.VMEM((g * tq, n_all), F32), pltpu.VMEM((g * tq, LANES), F32),
                        pltpu.VMEM((g * tq, n_all), BF16)],
        compiler_params=_cparams("arbitrary"),
        name="gqa_attention_ctx" if ctx_queries else "gqa_attention",
    )(*args)


def _router_kernel(x_ref, w_ref, bias_ref, tri_ref, idx_o, gate_o, rank_o, cnt_o, carry_ref):
    @pl.when(pl.program_id(0) == 0)
    def _():
        carry_ref[...] = jnp.zeros_like(carry_ref)

    tm = x_ref.shape[0] // TOKEN_ROWS
    x = _unpack_pair(_load_tokens(x_ref, tm))
    scores = _sigmoid(jnp.dot(x, w_ref[...], preferred_element_type=F32))
    lane = lax.broadcasted_iota(jnp.int32, (tm, LANES), 1)
    sel = jnp.where(lane < N_EXPERTS, scores + bias_ref[...], NEG_BIG)
    picks = []
    gates = []
    chosen = jnp.zeros((tm, LANES), F32)
    for _ in range(TOP_K):
        mx = jnp.max(sel, axis=-1, keepdims=True)
        idx = jnp.min(jnp.where(sel == mx, lane, LANES), axis=-1, keepdims=True)
        hit = lane == idx
        picks.append((idx, hit))
        gates.append(jnp.sum(jnp.where(hit, scores, 0.0), axis=-1, keepdims=True))
        chosen = chosen + hit.astype(F32)
        sel = jnp.where(hit, NEG_BIG, sel)
    gsum = gates[0]
    for gj in gates[1:]:
        gsum = gsum + gj
    before = jnp.dot(tri_ref[...], chosen.astype(BF16), preferred_element_type=F32) + carry_ref[...]
    idx_out = jnp.zeros((tm, LANES), jnp.int32)
    gate_out = jnp.zeros((tm, LANES), F32)
    rank_out = jnp.zeros((tm, LANES), jnp.int32)
    for j, (idx, hit) in enumerate(picks):
        rank = jnp.sum(jnp.where(hit, before, 0.0), axis=-1, keepdims=True).astype(jnp.int32)
        idx_out = jnp.where(lane == j, idx, idx_out)
        gate_out = jnp.where(lane == j, ROUTE_SCALE * gates[j] / gsum, gate_out)
        rank_out = jnp.where(lane == j, rank, rank_out)
    idx_o[...] = idx_out
    gate_o[...] = gate_out
    rank_o[...] = rank_out
    carry_ref[...] = carry_ref[...] + jnp.sum(chosen, axis=0, keepdims=True)
    cnt_o[...] = carry_ref[...].astype(jnp.int32)


def _router(vfp, router_w, bias):
    n = vfp.shape[0] // TOKEN_ROWS
    tm = ROUTER_TILE if n % ROUTER_TILE == 0 else 128
    d = router_w.shape[0]
    wpad = jnp.zeros((d, LANES), BF16).at[:, :N_EXPERTS].set(router_w.astype(BF16))
    bpad = jnp.zeros((1, LANES), F32).at[0, :N_EXPERTS].set(bias)
    tri = (lax.broadcasted_iota(jnp.int32, (tm, tm), 1) < lax.broadcasted_iota(jnp.int32, (tm, tm), 0)).astype(BF16)
    row = pl.BlockSpec((tm, LANES), lambda i: (i, 0))
    return pl.pallas_call(
        _router_kernel,
        out_shape=(jax.ShapeDtypeStruct((n, LANES), jnp.int32), jax.ShapeDtypeStruct((n, LANES), F32),
                   jax.ShapeDtypeStruct((n, LANES), jnp.int32), jax.ShapeDtypeStruct((1, LANES), jnp.int32)),
        grid=(n // tm,),
        in_specs=[
            pl.BlockSpec((tm * TOKEN_ROWS, LANES), lambda i: (i, 0)),
            pl.BlockSpec((d, LANES), lambda i: (0, 0)),
            pl.BlockSpec((1, LANES), lambda i: (0, 0)),
            pl.BlockSpec((tm, tm), lambda i: (0, 0)),
        ],
        out_specs=(row, row, row, pl.BlockSpec((1, LANES), lambda i: (0, 0))),
        scratch_shapes=[pltpu.VMEM((1, LANES), F32)],
        compiler_params=_cparams("arbitrary"),
        name="moe_router",
    )(vfp, wpad, bpad, tri)


def _token_tile(ref, t):
    return ref.at[pl.ds(pl.multiple_of(t * TOKEN_ROWS, TOKEN_ROWS), TOKEN_ROWS)]


def _dispatch_kernel(dest_ref, x_ref, xs_in_ref, xs_ref, sem):
    del xs_in_ref
    tm = x_ref.shape[0] // TOKEN_ROWS

    def copy(r, d):
        return pltpu.make_async_copy(_token_tile(x_ref, r), _token_tile(xs_ref, d), sem)

    def start(r, carry):
        for j in range(TOP_K):
            copy(r, dest_ref[0, 0, r * TOP_K + j]).start(priority=j % 2)
        return carry

    lax.fori_loop(0, tm, start, 0, unroll=2)

    def wait(r, carry):
        for _ in range(TOP_K):
            copy(0, 0).wait()
        return carry

    lax.fori_loop(0, tm, wait, 0)


def _dispatch(vfp, dest, xs0):
    n = vfp.shape[0] // TOKEN_ROWS
    tm = DISPATCH_TILE if n % DISPATCH_TILE == 0 else 128
    dest3 = dest.reshape(n // tm, 1, tm * TOP_K)
    return pl.pallas_call(
        _dispatch_kernel,
        out_shape=jax.ShapeDtypeStruct(xs0.shape, U32),
        grid=(n // tm,),
        in_specs=[
            pl.BlockSpec((1, 1, tm * TOP_K), lambda i: (i, 0, 0), memory_space=pltpu.SMEM),
            pl.BlockSpec((tm * TOKEN_ROWS, LANES), lambda i: (i, 0)),
            pl.BlockSpec(memory_space=pl.ANY),
        ],
        out_specs=pl.BlockSpec(memory_space=pl.ANY),
        scratch_shapes=[pltpu.SemaphoreType.DMA(())],
        input_output_aliases={2: 0},
        compiler_params=_cparams("arbitrary"),
        name="moe_dispatch",
    )(dest3, vfp, xs0)


def _ffn_kernel(be_ref, on_ref, x_ref, w1_ref, w3_ref, w2_ref, o_ref, w13b, w2b):
    i = pl.program_id(0)
    prev = be_ref[jnp.maximum(i - 1, 0)]
    ff = w1_ref.shape[2]

    @pl.when((i == 0) | (be_ref[i] != prev))
    def _():
        w13b[:, :ff] = w1_ref[0].astype(BF16)
        w13b[:, ff:] = w3_ref[0].astype(BF16)
        w2b[...] = w2_ref[0].astype(BF16)

    @pl.when(on_ref[i] != 0)
    def _():
        x = _unpack_pair(_load_tokens(x_ref, x_ref.shape[0] // TOKEN_ROWS))
        ac = jnp.dot(x, w13b[...], preferred_element_type=F32)
        a = ac[:, :ff]
        hmid = (a * _sigmoid(a) * ac[:, ff:]).astype(BF16)
        _store_tokens(o_ref, _pack_pair(jnp.dot(hmid, w2b[...], preferred_element_type=F32)))

    @pl.when(on_ref[i] == 0)
    def _():
        o_ref[...] = jnp.zeros_like(o_ref)


def _grouped_ffn(xs, blk_e, blk_on, w1, w3, w2, tm):
    n_blocks = blk_e.shape[0]
    d, ff = w1.shape[1], w1.shape[2]
    tile = pl.BlockSpec((tm * TOKEN_ROWS, LANES), lambda i, be, on: (i, 0))
    return pl.pallas_call(
        _ffn_kernel,
        out_shape=jax.ShapeDtypeStruct((n_blocks * tm * TOKEN_ROWS, LANES), U32),
        grid_spec=pltpu.PrefetchScalarGridSpec(
            num_scalar_prefetch=2,
            grid=(n_blocks,),
            in_specs=[
                tile,
                pl.BlockSpec((1, d, ff), lambda i, be, on: (be[i], 0, 0)),
                pl.BlockSpec((1, d, ff), lambda i, be, on: (be[i], 0, 0)),
                pl.BlockSpec((1, ff, d), lambda i, be, on: (be[i], 0, 0)),
            ],
            out_specs=tile,
            scratch_shapes=[pltpu.VMEM((d, 2 * ff), BF16), pltpu.VMEM((ff, d), BF16)],
        ),
        compiler_params=_cparams("arbitrary"),
        name="moe_ffn",
    )(blk_e, blk_on, xs, w1, w3, w2)


def _combine_kernel(dest_ref, dnext_ref, gate_ref, sh_ref, h_ref, gf_ref, g_ref, b_ref, ys_ref, o_ref, buf_ref, sem):
    tm = h_ref.shape[0]
    i = pl.program_id(0)
    slot = i % 2

    def copy(sl, r, j, d):
        return pltpu.make_async_copy(_token_tile(ys_ref, d), _token_tile(buf_ref.at[sl, j], r), sem.at[sl])

    def start_tile(idx_ref, sl):
        def start(r, carry):
            for j in range(TOP_K):
                copy(sl, r, j, idx_ref[0, 0, r * TOP_K + j]).start(priority=j % 2)
            return carry

        lax.fori_loop(0, tm, start, 0, unroll=2)

    @pl.when(i == 0)
    def _():
        start_tile(dest_ref, 0)

    @pl.when(i + 1 < pl.num_programs(0))
    def _():
        start_tile(dnext_ref, 1 - slot)

    def wait(r, carry):
        for _ in range(TOP_K):
            copy(slot, 0, 0, 0).wait()
        return carry

    lax.fori_loop(0, tm, wait, 0)

    f = _unpack_pair(_load_tokens(sh_ref, tm)).astype(F32)
    gate = gate_ref[...]
    for j in range(TOP_K):
        f = f + gate[:, j:j + 1] * _unpack_pair(_load_tokens(buf_ref.at[slot, j], tm)).astype(F32)
    o_ref[...] = _layer_norm(DEEPNORM_ALPHA * h_ref[...] + gf_ref[0] * f, g_ref[...], b_ref[...])


def _combine(ys, dest, gate, shared, h, mod3, ln_g, ln_b, dims):
    n, d = h.shape
    tm = COMBINE_TILE
    seq = _seq_of_tile(dims, tm)
    dest3 = dest.reshape(n // tm, 1, tm * TOP_K)
    return pl.pallas_call(
        _combine_kernel,
        out_shape=jax.ShapeDtypeStruct((n, d), F32),
        grid=(n // tm,),
        in_specs=[
            pl.BlockSpec((1, 1, tm * TOP_K), lambda i: (i, 0, 0), memory_space=pltpu.SMEM),
            pl.BlockSpec((1, 1, tm * TOP_K), lambda i: (jnp.minimum(i + 1, n // tm - 1), 0, 0),
                         memory_space=pltpu.SMEM),
            pl.BlockSpec((tm, LANES), lambda i: (i, 0)),
            pl.BlockSpec((tm * TOKEN_ROWS, LANES), lambda i: (i, 0)),
            pl.BlockSpec((tm, d), lambda i: (i, 0)),
            pl.BlockSpec((1, 1, d), lambda i: (seq(i) * 6 + 5, 0, 0)),
            pl.BlockSpec((1, d), lambda i: (0, 0)),
            pl.BlockSpec((1, d), lambda i: (0, 0)),
            pl.BlockSpec(memory_space=pl.ANY),
        ],
        out_specs=pl.BlockSpec((tm, d), lambda i: (i, 0)),
        scratch_shapes=[pltpu.VMEM((2, TOP_K, tm * TOKEN_ROWS, LANES), U32), pltpu.SemaphoreType.DMA((2,))],
        compiler_params=_cparams("arbitrary"),
        name="moe_combine_ln",
    )(dest3, dest3, gate, shared, h, mod3, ln_g.reshape(1, d), ln_b.reshape(1, d), ys)


def _moe(vfp, h, mod3, router_w, bias, w1, w3, w2, sw1, sw3, sw2, layer, ln_g, ln_b, dims, xs_buf):
    n = vfp.shape[0] // TOKEN_ROWS
    tmb = MOE_BLOCK
    idx, gate, rank, cnt = _router(vfp, router_w, bias)
    counts = cnt[0, :N_EXPERTS]
    padded = (counts + tmb - 1) // tmb * tmb
    pend = jnp.cumsum(padded)
    pstart = pend - padded
    n_blocks = (n * TOP_K + N_EXPERTS * (tmb - 1) + tmb - 1) // tmb
    blk_pos = jnp.arange(n_blocks, dtype=jnp.int32) * tmb
    blk_e = jnp.minimum(jnp.sum((pend[None, :] <= blk_pos[:, None]).astype(jnp.int32), axis=1), N_EXPERTS - 1)
    blk_on = (blk_pos < pend[-1]).astype(jnp.int32)
    e_sel = idx[:, :TOP_K]
    onehot = e_sel[:, :, None] == jnp.arange(N_EXPERTS, dtype=jnp.int32)[None, None, :]
    dest = rank[:, :TOP_K] + jnp.sum(jnp.where(onehot, pstart[None, None, :], 0), axis=-1)
    dest = dest.reshape(-1).astype(jnp.int32)
    if xs_buf is None:
        xs_buf = jnp.zeros((n_blocks * tmb * TOKEN_ROWS, LANES), U32)
    xs = _dispatch(vfp, dest, xs_buf)
    flat = lambda w: w.reshape((-1,) + w.shape[-2:])
    ys = _grouped_ffn(xs, blk_e + layer * N_EXPERTS, blk_on, flat(w1), flat(w3), flat(w2), tmb)
    ones = jnp.ones((n // tmb,), jnp.int32)
    shared = _grouped_ffn(vfp, jnp.full((n // tmb,), layer, jnp.int32), ones, sw1, sw3, sw2, tmb)
    return _combine(ys, dest, gate, shared, h, mod3, ln_g, ln_b, dims), xs


def _to_scan(rows, b):
    t = rows.shape[0] // b
    return rows.reshape(b, t, RW_HEADS, RW_HEAD).transpose(1, 3, 0, 2).reshape(t, RW_HEAD, b * RW_HEADS)


def _from_scan(y, b):
    t = y.shape[0]
    return y.reshape(t, RW_HEAD, b, RW_HEADS).transpose(2, 0, 3, 1).reshape(b * t, RW_WIDTH)


def _even_params(j, even_w_in, rw_mu, rw_w0, rw_w_up, rw_a0, rw_a_up, rw_g_up, rw_kk, rw_ka, rw_rk, rw_gn_g,
                 rw_gn_b, cv_w, cv_b, cv_ln_g, cv_ln_b):
    w3 = 3 * RW_WIDTH
    o_g, o_w, o_a = w3, w3 + GATE_LORA, w3 + GATE_LORA + 2 * DECAY_LORA
    o_cv = o_a + 2 * ICLR_LORA

    def relay(m):
        lead = m.shape[0]
        z = lambda k: jnp.zeros((lead, k), m.dtype)
        pieces = [m[:, :w3], m[:, o_g:o_g + GATE_LORA], z(LANES - GATE_LORA)]
        for d in range(2):
            pieces += [m[:, o_w + d * DECAY_LORA:o_w + (d + 1) * DECAY_LORA], z(LANES - DECAY_LORA)]
        for d in range(2):
            pieces += [m[:, o_a + d * ICLR_LORA:o_a + (d + 1) * ICLR_LORA], z(LANES - ICLR_LORA)]
        return jnp.concatenate(pieces, axis=1)

    w_in = even_w_in[j]
    w_pad = jnp.concatenate([relay(w_in), jnp.zeros((w_in.shape[0], 2 * 2048 - RW_STREAM_PAD), w_in.dtype),
                             w_in[:, o_cv:]], axis=1).astype(BF16)
    pad_rows = lambda m, k: jnp.concatenate([m, jnp.zeros((k - m.shape[0], m.shape[1]), m.dtype)], axis=0)
    head = jnp.arange(RW_WIDTH) // RW_HEAD
    vec = lambda m: m.reshape(1, -1)
    return {
        "w_in": w_pad,
        "mu": relay(rw_mu[j]),
        "w0": rw_w0[j], "a0": rw_a0[j],
        "w_up": jnp.stack([pad_rows(rw_w_up[j, d], LANES) for d in range(2)]).astype(BF16),
        "a_up": jnp.stack([pad_rows(rw_a_up[j, d], LANES) for d in range(2)]).astype(BF16),
        "g_up": pad_rows(rw_g_up[j], LANES).astype(BF16),
        "kk": vec(rw_kk[j]), "ka": vec(rw_ka[j]), "rk": vec(rw_rk[j]),
        "hs": (head[:, None] == head[None, :]).astype(BF16),
        "gn_g": vec(rw_gn_g[j]), "gn_b": vec(rw_gn_b[j]),
        "cv_w": pad_rows(cv_w[j], 32), "cv_b": vec(cv_b[j]), "cv_g": vec(cv_ln_g[j]), "cv_beta": vec(cv_ln_b[j]),
    }


def _even_mixer(h, mod3, prm, dims):
    p = _inproj(h, mod3, prm["w_in"], dims, 1, 0)
    r, v, kk, g, bon, dec, bvec, kd = _rwkv_features(p, prm, dims)
    b = dims[0]
    nl = b * dims[1]
    state = [jnp.zeros((RW_HEAD, RW_HEAD, LANES), F32)] * 2
    y_parts = []
    for rows in (slice(nl, None), slice(0, nl)):
        shared = [_to_scan(x[rows], b) for x in (r, kk, v)]
        y_sum = None
        for d in range(2):
            per_dir = [_to_scan(x[d, rows], b) for x in (dec, bvec, kd)]
            y_sum, state[d] = _wkv_scan(*shared, *per_dir, state[d], reverse=(d == 1), y_prev=y_sum)
        y_parts.append(_from_scan(y_sum, b))
    rw = _rwkv_post(jnp.concatenate([y_parts[1], y_parts[0]], axis=0), bon, g, prm)
    cv = _conformer_conv(p, prm, dims)
    return [rw, cv]


def _attn_mixer(h, mod3, w_in, qn, kn, tables, dims, with_ctx):
    p = _inproj(h, mod3, w_in, dims, 1, 0)
    q, k, v = _qkv_post(p, tables[0], tables[1], qn, kn, dims)
    out = _attention(q, k, v, dims, ctx_queries=False)
    if not with_ctx:
        return [out]
    return [jnp.concatenate([out, _attention(q, k, v, dims, ctx_queries=True)], axis=0)]


def kernel(x, c, ctx, c_ctx, ada_w, ada_b, ln1_g, ln1_b, ln2_g, ln2_b, even_w_in, even_w_out, rw_mu, rw_w0, rw_w_up, rw_a0, rw_a_up, rw_g_up, rw_kk, rw_ka, rw_rk, rw_gn_g, rw_gn_b, cv_w, cv_b, cv_ln_g, cv_ln_b, odd_w_in, odd_w_out, q_norm, k_norm, moe_router, moe_bias, moe_w1, moe_w3, moe_w2, sh_w1, sh_w3, sh_w2):
    bsz, s_len, d = x.shape
    ctx_len = ctx.shape[1]
    dims = (bsz, s_len, ctx_len)
    assert bsz * RW_HEADS == LANES, "the WKV scan maps (batch, head) pairs onto the 128 lanes"
    n_lat = bsz * s_len
    depth = ada_w.shape[0]

    c16 = jnp.zeros((16, d), F32).at[:bsz].set(c).at[bsz].set(c_ctx)
    mod_all = _ada_all(c16, ada_w, ada_b)[:, :bsz + 1].reshape(depth, (bsz + 1) * 6, 1, d)
    tables = _rope_tables(s_len, min(ROW_TILE, ctx_len))

    h = jnp.concatenate([x.reshape(n_lat, d), ctx.reshape(bsz * ctx_len, d)], axis=0)
    xs_buf = None
    for layer in range(depth):
        j = layer // 2
        last = layer == depth - 1
        mod3 = mod_all[layer]
        if layer % 2 == 0:
            prm = _even_params(j, even_w_in, rw_mu, rw_w0, rw_w_up, rw_a0, rw_a_up, rw_g_up, rw_kk, rw_ka, rw_rk,
                               rw_gn_g, rw_gn_b, cv_w, cv_b, cv_ln_g, cv_ln_b)
            acts = _even_mixer(h, mod3, prm, dims)
            w_out = even_w_out[j].astype(BF16)
        else:
            acts = _attn_mixer(h, mod3, odd_w_in[j].astype(BF16), q_norm[j], k_norm[j], tables, dims,
                               with_ctx=not last)
            w_out = odd_w_out[j].astype(BF16)
        h, vfp = _outproj(acts, w_out, h, mod3, ln1_g[layer], ln1_b[layer], dims)
        h, xs_buf = _moe(vfp, h, mod3, moe_router[layer], moe_bias[layer], moe_w1, moe_w3, moe_w2, sh_w1, sh_w3,
                         sh_w2, layer, ln2_g[layer], ln2_b[layer], dims, xs_buf)
    return h[:n_lat].reshape(bsz, s_len, d)
```

```python
import functools
import math

import jax
import jax.numpy as jnp
from jax import lax
from jax.experimental import pallas as pl
from jax.experimental.pallas import tpu as pltpu

F32 = jnp.float32
BF16 = jnp.bfloat16
U32 = jnp.uint32

D_MODEL = 2048
DEPTH = 4
GRID_W = 64

RW_HEADS = 16
RW_HEAD = 64
RW_WIDTH = RW_HEADS * RW_HEAD
DECAY_LORA = 96
ICLR_LORA = 96
GATE_LORA = 64
RW_GN_EPS = 64e-5
CONV_WIDTH = 1024
CONV_K = 31

ATT_HEADS = 16
ATT_KV_HEADS = 4
ATT_HEAD = 128
ATT_GROUP = ATT_HEADS // ATT_KV_HEADS
ATT_Q = ATT_HEADS * ATT_HEAD
ATT_KV = ATT_KV_HEADS * ATT_HEAD
ROPE_THETA = 10000.0
ROPE_PAIRS = ATT_HEAD // 4
QK_EPS = 1e-6

N_EXPERTS = 64
TOP_K = 6
EXPERT_FF = 384
ROUTE_SCALE = 2.5

DEEPNORM_ALPHA = (2 * DEPTH) ** 0.25
LN_EPS = 1e-5

LANES = 128
SMALL_W = 640
RW_STREAM_PAD = 3 * RW_WIDTH + SMALL_W
EVEN_IN_PAD = 6144
CONV_COL_BLOCK = 2
EXP_M05 = math.exp(-0.5)
NEG_BIG = -1e30
VMEM_LIMIT = 56 * 1024 * 1024

ROW_TILE = 256
HALO = 16
SCAN_STEPS = 64
MOE_BLOCK = 512
ROUTER_TILE = 512
DISPATCH_TILE = 256
COMBINE_TILE = 256
ATTN_ROW_BLOCK = 128


def _cparams(*sem):
    return pltpu.CompilerParams(dimension_semantics=tuple(sem), vmem_limit_bytes=VMEM_LIMIT)


def _layer_norm(x, g, b):
    mu = jnp.mean(x, axis=-1, keepdims=True)
    xc = x - mu
    var = jnp.mean(xc * xc, axis=-1, keepdims=True)
    return xc * lax.rsqrt(var + LN_EPS) * g + b


def _sigmoid(x):
    return 1.0 / (1.0 + jnp.exp(-x))


def _pack_pair(x):
    k = x.shape[1] // 2
    lo = lax.bitcast_convert_type(x[:, :k].astype(BF16).astype(F32), U32) >> 16
    hi = lax.bitcast_convert_type(x[:, k:].astype(BF16).astype(F32), U32) & jnp.uint32(0xFFFF0000)
    return hi | lo


def _unpack_pair(w):
    lo = lax.bitcast_convert_type(w << 16, F32)
    hi = lax.bitcast_convert_type(w & jnp.uint32(0xFFFF0000), F32)
    return jnp.concatenate([lo, hi], axis=1).astype(BF16)


TOKEN_ROWS = 8


def _load_tokens(ref, n_tok):
    return jnp.concatenate([ref[pl.ds(s, n_tok, stride=TOKEN_ROWS), :] for s in range(TOKEN_ROWS)], axis=1)


def _store_tokens(ref, words):
    n_tok = words.shape[0]
    for s in range(TOKEN_ROWS):
        ref[pl.ds(s, n_tok, stride=TOKEN_ROWS), :] = words[:, s * LANES:(s + 1) * LANES]


def _col_tile(n, cap=1280):
    best = LANES
    for t in range(LANES, cap + 1, LANES):
        if n % t == 0:
            best = t
    return best


def _ada_kernel(c_ref, w_ref, b_ref, o_ref):
    c = c_ref[...]
    s = (c * _sigmoid(c)).astype(BF16)
    o_ref[0] = jnp.dot(s, w_ref[0].astype(BF16), preferred_element_type=F32) + b_ref[0]


def _ada_all(c16, ada_w, ada_b):
    depth, d, n6 = ada_w.shape
    tn = 1024
    return pl.pallas_call(
        _ada_kernel,
        out_shape=jax.ShapeDtypeStruct((depth, 16, n6), F32),
        grid=(depth, n6 // tn),
        in_specs=[
            pl.BlockSpec((16, d), lambda l, j: (0, 0)),
            pl.BlockSpec((1, d, tn), lambda l, j: (l, 0, j)),
            pl.BlockSpec((1, 1, tn), lambda l, j: (l, 0, j)),
        ],
        out_specs=pl.BlockSpec((1, 16, tn), lambda l, j: (l, 0, j)),
        compiler_params=_cparams("parallel", "parallel"),
        name="ada_mod",
    )(c16, ada_w, ada_b.reshape(depth, 1, n6))


def _inproj_kernel(x_ref, sc_ref, sh_ref, w_ref, o_ref, u_ref):
    @pl.when(pl.program_id(1) == 0)
    def _():
        u_ref[...] = (x_ref[...] * (1.0 + sc_ref[0]) + sh_ref[0]).astype(BF16)

    o_ref[...] = jnp.dot(u_ref[...], w_ref[...], preferred_element_type=F32).astype(o_ref.dtype)


def _inproj(h, mod3, w, dims, sc_idx, sh_idx):
    n, d = h.shape
    nout = w.shape[1]
    tm = min(512, dims[1])
    tn = _col_tile(nout)
    seq = _seq_of_tile(dims, tm)
    return pl.pallas_call(
        _inproj_kernel,
        out_shape=jax.ShapeDtypeStruct((n, nout), BF16),
        grid=(n // tm, nout // tn),
        in_specs=[
            pl.BlockSpec((tm, d), lambda i, j: (i, 0)),
            pl.BlockSpec((1, 1, d), lambda i, j: (seq(i) * 6 + sc_idx, 0, 0)),
            pl.BlockSpec((1, 1, d), lambda i, j: (seq(i) * 6 + sh_idx, 0, 0)),
            pl.BlockSpec((d, tn), lambda i, j: (0, j)),
        ],
        out_specs=pl.BlockSpec((tm, tn), lambda i, j: (i, j)),
        scratch_shapes=[pltpu.VMEM((tm, d), BF16)],
        compiler_params=_cparams("parallel", "arbitrary"),
        name="inproj",
    )(h, mod3, mod3, w)


def _seq_of_tile(dims, tm):
    b, s, _ = dims

    def seq(i):
        return jnp.minimum((i * tm) // s, b)

    return seq


def _outproj_kernel(*refs, splits):
    n_in = len(splits)
    a_refs = refs[:n_in]
    w_ref, h_ref, ga_ref, scf_ref, shf_ref, g_ref, b_ref, ho_ref, vf_ref = refs[n_in:]
    acc = None
    off = 0
    for a_ref, k in zip(a_refs, splits):
        part = jnp.dot(a_ref[...], w_ref[off:off + k, :], preferred_element_type=F32)
        acc = part if acc is None else acc + part
        off += k
    hn = _layer_norm(DEEPNORM_ALPHA * h_ref[...] + ga_ref[0] * acc, g_ref[...], b_ref[...])
    ho_ref[...] = hn
    _store_tokens(vf_ref, _pack_pair(hn * (1.0 + scf_ref[0]) + shf_ref[0]))


def _outproj(acts, w, h, mod3, ln_g, ln_b, dims):
    n, d = acts[0].shape[0], h.shape[1]
    tm = min(256, dims[1])
    seq = _seq_of_tile(dims, tm)
    splits = tuple(a.shape[1] for a in acts)
    in_specs = [pl.BlockSpec((tm, k), lambda i: (i, 0)) for k in splits]
    in_specs += [
        pl.BlockSpec(w.shape, lambda i: (0, 0)),
        pl.BlockSpec((tm, d), lambda i: (i, 0)),
        pl.BlockSpec((1, 1, d), lambda i: (seq(i) * 6 + 2, 0, 0)),
        pl.BlockSpec((1, 1, d), lambda i: (seq(i) * 6 + 4, 0, 0)),
        pl.BlockSpec((1, 1, d), lambda i: (seq(i) * 6 + 3, 0, 0)),
        pl.BlockSpec((1, d), lambda i: (0, 0)),
        pl.BlockSpec((1, d), lambda i: (0, 0)),
    ]
    return pl.pallas_call(
        functools.partial(_outproj_kernel, splits=splits),
        out_shape=(jax.ShapeDtypeStruct((n, d), F32), jax.ShapeDtypeStruct((n * TOKEN_ROWS, LANES), U32)),
        grid=(n // tm,),
        in_specs=in_specs,
        out_specs=(pl.BlockSpec((tm, d), lambda i: (i, 0)), pl.BlockSpec((tm * TOKEN_ROWS, LANES), lambda i: (i, 0))),
        compiler_params=_cparams("parallel"),
        name="outproj_ln",
    )(*acts, w, h, mod3, mod3, mod3, ln_g.reshape(1, d), ln_b.reshape(1, d))


def _tile_flags(dims, tb):
    b, s, ctx = dims
    n = b * (s + ctx)
    nlt = (b * s) // tb
    nbl = s // tb
    nbc = ctx // tb

    def first_last(i):
        il = i % nbl
        ic = (i - nlt) % nbc
        lat = i < nlt
        first = jnp.where(lat, il == 0, ic == 0)
        last = jnp.where(lat, il == nbl - 1, ic == nbc - 1)
        return first, last

    def prev_blk(i):
        return jnp.maximum((i * tb) // HALO - 1, 0)

    def next_blk(i):
        return jnp.minimum(((i + 1) * tb) // HALO, n // HALO - 1)

    return first_last, prev_blk, next_blk


def _head_sum(x, hs_ref):
    hi = x.astype(BF16)
    lo = (x - hi.astype(F32)).astype(BF16)
    hs = hs_ref[...]
    return jnp.dot(hi, hs, preferred_element_type=F32) + jnp.dot(lo, hs, preferred_element_type=F32)


def _feat_kernel(p_ref, pp_ref, pn_ref, mu_ref, w0_ref, a0_ref, wup_ref, aup_ref, gup_ref, kk_ref, ka_ref,
                 rk_ref, hs_ref, r_o, v_o, kkn_o, g_o, bon_o, dec_o, b_o, kd_o, buf_ref, *, tb, first_last):
    first, last = first_last(pl.program_id(0))
    w = RW_WIDTH
    buf_ref[8:8 + tb, :] = p_ref[...].astype(F32)
    buf_ref[7:8, :] = jnp.where(first, 0.0, pp_ref[HALO - 1:HALO, :].astype(F32))
    buf_ref[8 + tb:9 + tb, :] = jnp.where(last, 0.0, pn_ref[0:1, :].astype(F32))

    def shifted(c0, c1):
        cur = buf_ref[8:8 + tb, c0:c1]
        prev = buf_ref[7:7 + tb, c0:c1]
        nxt = buf_ref[9:9 + tb, c0:c1]
        return cur + mu_ref[0:1, c0:c1] * (prev - cur) + mu_ref[1:2, c0:c1] * (nxt - cur)

    r = shifted(0, w)
    k = shifted(w, 2 * w)
    v = shifted(2 * w, 3 * w)
    sm = shifted(3 * w, 3 * w + SMALL_W)

    g = jnp.dot(_sigmoid(sm[:, 0:LANES]).astype(BF16), gup_ref[...], preferred_element_type=F32)
    kkr = k * kk_ref[...]
    kk = kkr * lax.rsqrt(_head_sum(kkr * kkr, hs_ref) + 1e-12)

    ksum = None
    for d in range(2):
        wd = sm[:, LANES * (1 + d):LANES * (2 + d)]
        ad = sm[:, LANES * (3 + d):LANES * (4 + d)]
        xw = w0_ref[d:d + 1, :] + jnp.dot(jnp.tanh(wd).astype(BF16), wup_ref[d], preferred_element_type=F32)
        dec_o[d] = jnp.exp(-EXP_M05 * _sigmoid(xw))
        a = _sigmoid(a0_ref[d:d + 1, :] + jnp.dot(ad.astype(BF16), aup_ref[d], preferred_element_type=F32))
        b_o[d] = (a * kk).astype(BF16)
        kd = k * (1.0 + (a - 1.0) * ka_ref[...])
        kd_o[d] = kd.astype(BF16)
        ksum = kd if ksum is None else ksum + kd

    bon_o[...] = (_head_sum(r * ksum * rk_ref[...], hs_ref) * v).astype(BF16)
    r_o[...] = r.astype(BF16)
    v_o[...] = v.astype(BF16)
    kkn_o[...] = kk.astype(BF16)
    g_o[...] = g.astype(BF16)


def _rwkv_features(p, prm, dims):
    n = p.shape[0]
    tb = min(ROW_TILE, dims[2])
    first_last, prev_blk, next_blk = _tile_flags(dims, tb)
    w = RW_WIDTH
    pw = RW_STREAM_PAD
    full2 = lambda shape: pl.BlockSpec(shape, lambda i: (0, 0))
    full3 = lambda shape: pl.BlockSpec(shape, lambda i: (0, 0, 0))
    row = pl.BlockSpec((tb, w), lambda i: (i, 0))
    row2 = pl.BlockSpec((2, tb, w), lambda i: (0, i, 0))
    sd = lambda dt: jax.ShapeDtypeStruct((n, w), dt)
    sd2 = lambda dt: jax.ShapeDtypeStruct((2, n, w), dt)
    return pl.pallas_call(
        functools.partial(_feat_kernel, tb=tb, first_last=first_last),
        out_shape=(sd(BF16), sd(BF16), sd(BF16), sd(BF16), sd(BF16), sd2(F32), sd2(BF16), sd2(BF16)),
        grid=(n // tb,),
        in_specs=[
            pl.BlockSpec((tb, pw), lambda i: (i, 0)),
            pl.BlockSpec((HALO, pw), lambda i: (prev_blk(i), 0)),
            pl.BlockSpec((HALO, pw), lambda i: (next_blk(i), 0)),
            full2((2, pw)), full2((2, w)), full2((2, w)),
            full3((2, LANES, w)), full3((2, LANES, w)), full2((LANES, w)),
            full2((1, w)), full2((1, w)), full2((1, w)), full2((w, w)),
        ],
        out_specs=(row, row, row, row, row, row2, row2, row2),
        scratch_shapes=[pltpu.VMEM((tb + 16, pw), F32)],
        compiler_params=_cparams("parallel"),
        name="rwkv_features",
    )(p, p, p, prm["mu"], prm["w0"], prm["a0"], prm["w_up"], prm["a_up"], prm["g_up"], prm["kk"], prm["ka"],
      prm["rk"], prm["hs"])


def _wkv_kernel(r_ref, kk_ref, v_ref, w_ref, b_ref, kd_ref, s0_ref, *refs, steps, reverse, add_prev):
    yp_ref = refs[0] if add_prev else None
    y_ref, st_ref, s_ref, rf, kkf, vf, bf, kdf = refs[1:] if add_prev else refs
    i = pl.program_id(0)

    @pl.when(i == 0)
    def _():
        s_ref[...] = s0_ref[...]

    for src, dst in ((r_ref, rf), (kk_ref, kkf), (v_ref, vf), (b_ref, bf), (kd_ref, kdf)):
        dst[...] = src[...].astype(F32)

    nk = RW_HEAD
    half = nk // 2
    t_first = steps - 1 if reverse else 0

    def s_dot_kk(t, lo):
        acc = jnp.zeros((half, LANES), F32)
        for k in range(nk):
            acc = acc + s_ref[k, lo:lo + half, :] * kkf[t, pl.ds(k, 1), :]
        return acc

    def step(j, sks):
        t = steps - 1 - j if reverse else j
        t_next = jnp.maximum(t - 1, 0) if reverse else jnp.minimum(t + 1, steps - 1)
        sks_next = []
        for sk, lo in zip(sks, (0, half)):
            vt = vf[t, lo:lo + half, :]
            y = jnp.zeros((half, LANES), F32)
            sk_next = jnp.zeros((half, LANES), F32)
            for k in range(nk):
                s = (s_ref[k, lo:lo + half, :] * w_ref[t, pl.ds(k, 1), :] - sk * bf[t, pl.ds(k, 1), :]
                     + vt * kdf[t, pl.ds(k, 1), :])
                s_ref[k, lo:lo + half, :] = s
                y = y + s * rf[t, pl.ds(k, 1), :]
                sk_next = sk_next + s * kkf[t_next, pl.ds(k, 1), :]
            y_ref[t, lo:lo + half, :] = y + yp_ref[t, lo:lo + half, :] if add_prev else y
            sks_next.append(sk_next)
        return tuple(sks_next)

    lax.fori_loop(0, steps, step, (s_dot_kk(t_first, 0), s_dot_kk(t_first, half)))

    @pl.when(i == pl.num_programs(0) - 1)
    def _():
        st_ref[...] = s_ref[...]


def _wkv_scan(r, kk, v, w, b, kd, s0, reverse, y_prev=None):
    t_all = r.shape[0]
    extra = [] if y_prev is None else [y_prev]
    steps = min(SCAN_STEPS, t_all)
    nblk = t_all // steps
    blk = (lambda i: (nblk - 1 - i, 0, 0)) if reverse else (lambda i: (i, 0, 0))
    seq = pl.BlockSpec((steps, RW_HEAD, LANES), blk)
    state = pl.BlockSpec((RW_HEAD, RW_HEAD, LANES), lambda i: (0, 0, 0))
    buf = pltpu.VMEM((steps, RW_HEAD, LANES), F32)
    return pl.pallas_call(
        functools.partial(_wkv_kernel, steps=steps, reverse=reverse, add_prev=y_prev is not None),
        out_shape=(jax.ShapeDtypeStruct((t_all, RW_HEAD, LANES), F32),
                   jax.ShapeDtypeStruct((RW_HEAD, RW_HEAD, LANES), F32)),
        grid=(nblk,),
        in_specs=[seq, seq, seq, seq, seq, seq, state] + [seq] * len(extra),
        out_specs=(seq, state),
        scratch_shapes=[pltpu.VMEM((RW_HEAD, RW_HEAD, LANES), F32), buf, buf, buf, buf, buf],
        compiler_params=_cparams("arbitrary"),
        name="wkv_scan_bwd" if reverse else "wkv_scan_fwd",
    )(r, kk, v, w, b, kd, s0, *extra)


def _rwpost_kernel(y_ref, bon_ref, g_ref, gng_ref, gnb_ref, hs_ref, o_ref):
    y = y_ref[...]
    inv = 1.0 / RW_HEAD
    yc = y - _head_sum(y, hs_ref) * inv
    var = _head_sum(yc * yc, hs_ref) * inv
    yn = yc * lax.rsqrt(var + RW_GN_EPS) * gng_ref[...] + gnb_ref[...]
    o_ref[...] = ((yn + bon_ref[...].astype(F32)) * g_ref[...].astype(F32)).astype(BF16)


def _rwkv_post(y, bon, g, prm):
    n, w = y.shape
    tb = 256
    row = pl.BlockSpec((tb, w), lambda i: (i, 0))
    vec = pl.BlockSpec((1, w), lambda i: (0, 0))
    return pl.pallas_call(
        _rwpost_kernel,
        out_shape=jax.ShapeDtypeStruct((n, w), BF16),
        grid=(n // tb,),
        in_specs=[row, row, row, vec, vec, pl.BlockSpec((w, w), lambda i: (0, 0))],
        out_specs=row,
        compiler_params=_cparams("parallel"),
        name="rwkv_post",
    )(y, bon, g, prm["gn_g"], prm["gn_b"], prm["hs"])


def _conv_kernel(p_ref, pp_ref, pn_ref, w_ref, b_ref, g_ref, beta_ref, o_ref, z_ref, *, tb, first_last):
    first, last = first_last(pl.program_id(0))
    c = CONV_WIDTH

    def glu(x):
        x = x.astype(F32)
        return x[:, :c] * _sigmoid(x[:, c:])

    z_ref[HALO:HALO + tb, :] = glu(p_ref[...])
    z_ref[0:HALO, :] = jnp.where(first, 0.0, glu(pp_ref[...]))
    z_ref[HALO + tb:2 * HALO + tb, :] = jnp.where(last, 0.0, glu(pn_ref[...]))
    acc = jnp.zeros((tb, c), F32)
    base = HALO - CONV_K // 2
    for j in range(CONV_K):
        acc = acc + z_ref[base + j:base + j + tb, :] * w_ref[j:j + 1, :]
    zn = _layer_norm(acc + b_ref[...], g_ref[...], beta_ref[...])
    o_ref[...] = (zn * _sigmoid(zn)).astype(BF16)


def _conformer_conv(p, prm, dims):
    n = p.shape[0]
    tb = min(ROW_TILE, dims[2])
    first_last, prev_blk, next_blk = _tile_flags(dims, tb)
    c = CONV_WIDTH
    cb = CONV_COL_BLOCK
    vec = pl.BlockSpec((1, c), lambda i: (0, 0))
    return pl.pallas_call(
        functools.partial(_conv_kernel, tb=tb, first_last=first_last),
        out_shape=jax.ShapeDtypeStruct((n, c), BF16),
        grid=(n // tb,),
        in_specs=[
            pl.BlockSpec((tb, 2 * c), lambda i: (i, cb)),
            pl.BlockSpec((HALO, 2 * c), lambda i: (prev_blk(i), cb)),
            pl.BlockSpec((HALO, 2 * c), lambda i: (next_blk(i), cb)),
            pl.BlockSpec((32, c), lambda i: (0, 0)), vec, vec, vec,
        ],
        out_specs=pl.BlockSpec((tb, c), lambda i: (i, 0)),
        scratch_shapes=[pltpu.VMEM((tb + 2 * HALO, c), F32)],
        compiler_params=_cparams("parallel"),
        name="conformer_conv",
    )(p, p, p, prm["cv_w"], prm["cv_b"], prm["cv_g"], prm["cv_beta"])


def _qkv_kernel(p_ref, cos_ref, sin_ref, qn_ref, kn_ref, q_o, k_o, v_o):
    lane = lax.broadcasted_iota(jnp.int32, (1, ATT_HEAD), 1)
    first_half = (lane % 64) < 32
    cosv = cos_ref[...]
    sinv = sin_ref[...]

    def norm_rope(x, gain, scale):
        x = x.astype(F32)
        xn = x * lax.rsqrt(jnp.mean(x * x, axis=-1, keepdims=True) + QK_EPS) * gain
        swapped = jnp.where(first_half, pltpu.roll(xn, ATT_HEAD - 32, axis=1), pltpu.roll(xn, 32, axis=1))
        return (xn * cosv + swapped * sinv) * scale

    hd = ATT_HEAD
    ones = jnp.ones((p_ref.shape[0], hd), BF16)
    for h in range(ATT_HEADS):
        q_o[h] = norm_rope(p_ref[:, h * hd:(h + 1) * hd], qn_ref[...], 1.0 / math.sqrt(ATT_HEAD)).astype(BF16)
    for h in range(ATT_KV_HEADS):
        k_o[h] = norm_rope(p_ref[:, ATT_Q + h * hd:ATT_Q + (h + 1) * hd], kn_ref[...], 1.0).T.astype(BF16)
        v_o[h, :, :hd] = p_ref[:, ATT_Q + ATT_KV + h * hd:ATT_Q + ATT_KV + (h + 1) * hd]
        v_o[h, :, hd:] = ones


def _qkv_post(p, cos_t, sin_t, qn, kn, dims):
    b, s, ctx = dims
    n = p.shape[0]
    tb = min(ROW_TILE, ctx)
    nlt = (b * s) // tb
    nbl = s // tb
    hd = ATT_HEAD

    def tab(i):
        return jnp.where(i < nlt, i % nbl, nbl)

    return pl.pallas_call(
        _qkv_kernel,
        out_shape=(jax.ShapeDtypeStruct((ATT_HEADS, n, hd), BF16),
                   jax.ShapeDtypeStruct((ATT_KV_HEADS, hd, n), BF16),
                   jax.ShapeDtypeStruct((ATT_KV_HEADS, n, 2 * hd), BF16)),
        grid=(n // tb,),
        in_specs=[
            pl.BlockSpec((tb, p.shape[1]), lambda i: (i, 0)),
            pl.BlockSpec((tb, hd), lambda i: (tab(i), 0)),
            pl.BlockSpec((tb, hd), lambda i: (tab(i), 0)),
            pl.BlockSpec((1, hd), lambda i: (0, 0)),
            pl.BlockSpec((1, hd), lambda i: (0, 0)),
        ],
        out_specs=(pl.BlockSpec((ATT_HEADS, tb, hd), lambda i: (0, i, 0)),
                   pl.BlockSpec((ATT_KV_HEADS, hd, tb), lambda i: (0, 0, i)),
                   pl.BlockSpec((ATT_KV_HEADS, tb, 2 * hd), lambda i: (0, i, 0))),
        compiler_params=_cparams("parallel"),
        name="qkv_norm_rope",
    )(p, cos_t, sin_t, qn.reshape(1, hd), kn.reshape(1, hd))


def _rope_tables(s, pad_rows):
    rows_n = s // GRID_W
    row = jnp.repeat(jnp.arange(rows_n), GRID_W)
    col = jnp.tile(jnp.arange(GRID_W), rows_n)
    inv = ROPE_THETA ** (-jnp.arange(ROPE_PAIRS, dtype=F32) / ROPE_PAIRS)
    ang_r = row[:, None] * inv
    ang_c = col[:, None] * inv
    cos_t = jnp.concatenate([jnp.cos(ang_r), jnp.cos(ang_r), jnp.cos(ang_c), jnp.cos(ang_c)], axis=1)
    sin_t = jnp.concatenate([-jnp.sin(ang_r), jnp.sin(ang_r), -jnp.sin(ang_c), jnp.sin(ang_c)], axis=1)
    cos_t = jnp.concatenate([cos_t, jnp.ones((pad_rows, ATT_HEAD), F32)], axis=0)
    sin_t = jnp.concatenate([sin_t, jnp.zeros((pad_rows, ATT_HEAD), F32)], axis=0)
    return cos_t, sin_t


def _attn_kernel(q_ref, *refs, seg_lens, kc):
    n_seg = len(seg_lens)
    o_ref, s_ref, mx_ref, p_ref = refs[2 * n_seg:]
    g, tq, hd = q_ref.shape

    @pl.when(pl.program_id(0) == 0)
    def _():
        p_ref[...] = jnp.zeros_like(p_ref)

    acc = None
    off = 0
    for si, n_keys in enumerate(seg_lens):
        pv = jnp.dot(p_ref[:, off:off + n_keys], refs[2 * si + 1][0], preferred_element_type=F32)
        acc = pv if acc is None else acc + pv
        off += n_keys

    q = q_ref[...].reshape(g * tq, hd)
    chunks = []
    off = 0
    for si, n_keys in enumerate(seg_lens):
        for c0 in range(0, n_keys, kc):
            c = min(kc, n_keys - c0)
            chunks.append((si, c0, c, off))
            off += c
    for n, (si, c0, c, off) in enumerate(chunks):
        s = jnp.dot(q, refs[2 * si][0, :, c0:c0 + c], preferred_element_type=F32)
        s_ref[:, off:off + c] = s
        part = s[:, :LANES]
        for j in range(1, c // LANES):
            part = jnp.maximum(part, s[:, j * LANES:(j + 1) * LANES])
        mx_ref[...] = part if n == 0 else jnp.maximum(mx_ref[...], part)

    o = acc[:, :hd] / jnp.maximum(acc[:, hd:], 1e-30)
    for gi in range(g):
        o_ref[:, gi * hd:(gi + 1) * hd] = o[gi * tq:(gi + 1) * tq, :].astype(o_ref.dtype)

    mx_ref[...] = jnp.broadcast_to(jnp.max(mx_ref[...], axis=-1, keepdims=True), mx_ref.shape)
    rb = min(ATTN_ROW_BLOCK, g * tq)

    def exp_rows(i, carry):
        r0 = pl.multiple_of(i * rb, rb)
        mb = mx_ref[pl.ds(r0, rb), :]
        for _, _, c, off in chunks:
            e = jnp.exp(s_ref[pl.ds(r0, rb), off:off + c] - jnp.concatenate([mb] * (c // LANES), axis=1))
            p_ref[pl.ds(r0, rb), off:off + c] = e.astype(BF16)
        return carry

    lax.fori_loop(0, (g * tq) // rb, exp_rows, 0)


def _attention(q, kt, v1, dims, ctx_queries):
    b, s, ctx = dims
    hd = ATT_HEAD
    g = ATT_GROUP
    kvh = ATT_KV_HEADS
    nl = b * s
    if ctx_queries:
        tq, nq, rows, q_base = ctx, 1, b * ctx, nl // ctx
        segs = [(ctx, nl // ctx)]
    else:
        tq = min(256, s)
        nq, rows, q_base = s // tq, nl, 0
        segs = [(s, 0), (ctx, nl // ctx)]
    n_tiles = b * kvh * nq

    def split(tile):
        return tile // (kvh * nq), (tile // nq) % kvh, tile % nq

    cur = lambda t: split(jnp.minimum(t, n_tiles - 1))
    prev = lambda t: split(jnp.maximum(t - 1, 0))

    def q_map(t):
        bi, kv, qi = cur(t)
        return kv, q_base + bi * nq + qi, 0

    def o_map(t):
        bi, kv, qi = prev(t)
        return bi * nq + qi, kv

    in_specs = [pl.BlockSpec((g, tq, hd), q_map)]
    args = [q]
    for n_keys, base in segs:
        def k_map(t, base=base):
            bi, kv, _ = cur(t)
            return kv, 0, base + bi

        def v_map(t, base=base):
            bi, kv, _ = prev(t)
            return kv, base + bi, 0

        in_specs += [pl.BlockSpec((1, hd, n_keys), k_map), pl.BlockSpec((1, n_keys, 2 * hd), v_map)]
        args += [kt, v1]
    n_all = sum(n for n, _ in segs)
    return pl.pallas_call(
        functools.partial(_attn_kernel, seg_lens=tuple(n for n, _ in segs), kc=512),
        out_shape=jax.ShapeDtypeStruct((rows, ATT_Q), BF16),
        grid=(n_tiles + 1,),
        in_specs=in_specs,
        out_specs=pl.BlockSpec((tq, g * hd), o_map),
        scratch_shapes=[pltpu.VMEM((g * tq, n_all), F32), pltpu.VMEM((g * tq, LANES), F32),
                        pltpu.VMEM((g * tq, n_all), BF16)],
        compiler_params=_cparams("arbitrary"),
        name="gqa_attention_ctx" if ctx_queries else "gqa_attention",
    )(*args)


def _router_kernel(x_ref, w_ref, bias_ref, tri_ref, idx_o, gate_o, rank_o, cnt_o, carry_ref):
    @pl.when(pl.program_id(0) == 0)
    def _():
        carry_ref[...] = jnp.zeros_like(carry_ref)

    tm = x_ref.shape[0] // TOKEN_ROWS
    x = _unpack_pair(_load_tokens(x_ref, tm))
    scores = _sigmoid(jnp.dot(x, w_ref[...], preferred_element_type=F32))
    lane = lax.broadcasted_iota(jnp.int32, (tm, LANES), 1)
    sel = jnp.where(lane < N_EXPERTS, scores + bias_ref[...], NEG_BIG)
    picks = []
    gates = []
    chosen = jnp.zeros((tm, LANES), F32)
    for _ in range(TOP_K):
        mx = jnp.max(sel, axis=-1, keepdims=True)
        idx = jnp.min(jnp.where(sel == mx, lane, LANES), axis=-1, keepdims=True)
        hit = lane == idx
        picks.append((idx, hit))
        gates.append(jnp.sum(jnp.where(hit, scores, 0.0), axis=-1, keepdims=True))
        chosen = chosen + hit.astype(F32)
        sel = jnp.where(hit, NEG_BIG, sel)
    gsum = gates[0]
    for gj in gates[1:]:
        gsum = gsum + gj
    before = jnp.dot(tri_ref[...], chosen.astype(BF16), preferred_element_type=F32) + carry_ref[...]
    idx_out = jnp.zeros((tm, LANES), jnp.int32)
    gate_out = jnp.zeros((tm, LANES), F32)
    rank_out = jnp.zeros((tm, LANES), jnp.int32)
    for j, (idx, hit) in enumerate(picks):
        rank = jnp.sum(jnp.where(hit, before, 0.0), axis=-1, keepdims=True).astype(jnp.int32)
        idx_out = jnp.where(lane == j, idx, idx_out)
        gate_out = jnp.where(lane == j, ROUTE_SCALE * gates[j] / gsum, gate_out)
        rank_out = jnp.where(lane == j, rank, rank_out)
    idx_o[...] = idx_out
    gate_o[...] = gate_out
    rank_o[...] = rank_out
    carry_ref[...] = carry_ref[...] + jnp.sum(chosen, axis=0, keepdims=True)
    cnt_o[...] = carry_ref[...].astype(jnp.int32)


def _router(vfp, router_w, bias):
    n = vfp.shape[0] // TOKEN_ROWS
    tm = ROUTER_TILE if n % ROUTER_TILE == 0 else 128
    d = router_w.shape[0]
    wpad = jnp.zeros((d, LANES), BF16).at[:, :N_EXPERTS].set(router_w.astype(BF16))
    bpad = jnp.zeros((1, LANES), F32).at[0, :N_EXPERTS].set(bias)
    tri = (lax.broadcasted_iota(jnp.int32, (tm, tm), 1) < lax.broadcasted_iota(jnp.int32, (tm, tm), 0)).astype(BF16)
    row = pl.BlockSpec((tm, LANES), lambda i: (i, 0))
    return pl.pallas_call(
        _router_kernel,
        out_shape=(jax.ShapeDtypeStruct((n, LANES), jnp.int32), jax.ShapeDtypeStruct((n, LANES), F32),
                   jax.ShapeDtypeStruct((n, LANES), jnp.int32), jax.ShapeDtypeStruct((1, LANES), jnp.int32)),
        grid=(n // tm,),
        in_specs=[
            pl.BlockSpec((tm * TOKEN_ROWS, LANES), lambda i: (i, 0)),
            pl.BlockSpec((d, LANES), lambda i: (0, 0)),
            pl.BlockSpec((1, LANES), lambda i: (0, 0)),
            pl.BlockSpec((tm, tm), lambda i: (0, 0)),
        ],
        out_specs=(row, row, row, pl.BlockSpec((1, LANES), lambda i: (0, 0))),
        scratch_shapes=[pltpu.VMEM((1, LANES), F32)],
        compiler_params=_cparams("arbitrary"),
        name="moe_router",
    )(vfp, wpad, bpad, tri)


def _token_tile(ref, t):
    return ref.at[pl.ds(pl.multiple_of(t * TOKEN_ROWS, TOKEN_ROWS), TOKEN_ROWS)]


def _dispatch_kernel(dest_ref, x_ref, xs_in_ref, xs_ref, sem):
    del xs_in_ref
    tm = x_ref.shape[0] // TOKEN_ROWS

    def copy(r, d):
        return pltpu.make_async_copy(_token_tile(x_ref, r), _token_tile(xs_ref, d), sem)

    def start(r, carry):
        for j in range(TOP_K):
            copy(r, dest_ref[0, 0, r * TOP_K + j]).start(priority=j % 2)
        return carry

    lax.fori_loop(0, tm, start, 0, unroll=2)

    done = xs_ref.at[pl.ds(0, tm * TOP_K * TOKEN_ROWS)]
    pltpu.make_async_copy(done, done, sem).wait()


def _dispatch(vfp, dest, xs0):
    n = vfp.shape[0] // TOKEN_ROWS
    tm = DISPATCH_TILE if n % DISPATCH_TILE == 0 else 128
    dest3 = dest.reshape(n // tm, 1, tm * TOP_K)
    return pl.pallas_call(
        _dispatch_kernel,
        out_shape=jax.ShapeDtypeStruct(xs0.shape, U32),
        grid=(n // tm,),
        in_specs=[
            pl.BlockSpec((1, 1, tm * TOP_K), lambda i: (i, 0, 0), memory_space=pltpu.SMEM),
            pl.BlockSpec((tm * TOKEN_ROWS, LANES), lambda i: (i, 0)),
            pl.BlockSpec(memory_space=pl.ANY),
        ],
        out_specs=pl.BlockSpec(memory_space=pl.ANY),
        scratch_shapes=[pltpu.SemaphoreType.DMA(())],
        input_output_aliases={2: 0},
        compiler_params=_cparams("arbitrary"),
        name="moe_dispatch",
    )(dest3, vfp, xs0)


def _ffn_kernel(be_ref, on_ref, x_ref, w1_ref, w3_ref, w2_ref, o_ref, w13b, w2b):
    i = pl.program_id(0)
    prev = be_ref[jnp.maximum(i - 1, 0)]
    ff = w1_ref.shape[2]

    @pl.when((i == 0) | (be_ref[i] != prev))
    def _():
        w13b[:, :ff] = w1_ref[0].astype(BF16)
        w13b[:, ff:] = w3_ref[0].astype(BF16)
        w2b[...] = w2_ref[0].astype(BF16)

    @pl.when(on_ref[i] != 0)
    def _():
        x = _unpack_pair(_load_tokens(x_ref, x_ref.shape[0] // TOKEN_ROWS))
        ac = jnp.dot(x, w13b[...], preferred_element_type=F32)
        a = ac[:, :ff]
        hmid = (a * _sigmoid(a) * ac[:, ff:]).astype(BF16)
        _store_tokens(o_ref, _pack_pair(jnp.dot(hmid, w2b[...], preferred_element_type=F32)))

    @pl.when(on_ref[i] == 0)
    def _():
        o_ref[...] = jnp.zeros_like(o_ref)


def _grouped_ffn(xs, blk_e, blk_on, w1, w3, w2, tm):
    n_blocks = blk_e.shape[0]
    d, ff = w1.shape[1], w1.shape[2]
    tile = pl.BlockSpec((tm * TOKEN_ROWS, LANES), lambda i, be, on: (i, 0))
    return pl.pallas_call(
        _ffn_kernel,
        out_shape=jax.ShapeDtypeStruct((n_blocks * tm * TOKEN_ROWS, LANES), U32),
        grid_spec=pltpu.PrefetchScalarGridSpec(
            num_scalar_prefetch=2,
            grid=(n_blocks,),
            in_specs=[
                tile,
                pl.BlockSpec((1, d, ff), lambda i, be, on: (be[i], 0, 0)),
                pl.BlockSpec((1, d, ff), lambda i, be, on: (be[i], 0, 0)),
                pl.BlockSpec((1, ff, d), lambda i, be, on: (be[i], 0, 0)),
            ],
            out_specs=tile,
            scratch_shapes=[pltpu.VMEM((d, 2 * ff), BF16), pltpu.VMEM((ff, d), BF16)],
        ),
        compiler_params=_cparams("arbitrary"),
        name="moe_ffn",
    )(blk_e, blk_on, xs, w1, w3, w2)


def _combine_kernel(dest_ref, dnext_ref, gate_ref, sh_ref, h_ref, gf_ref, g_ref, b_ref, ys_ref, o_ref, buf_ref, sem):
    tm = h_ref.shape[0]
    i = pl.program_id(0)
    slot = i % 2

    def copy(sl, r, j, d):
        return pltpu.make_async_copy(_token_tile(ys_ref, d), _token_tile(buf_ref.at[sl, j], r), sem.at[sl])

    def start_tile(idx_ref, sl):
        def start(r, carry):
            for j in range(TOP_K):
                copy(sl, r, j, idx_ref[0, 0, r * TOP_K + j]).start(priority=j % 2)
            return carry

        lax.fori_loop(0, tm, start, 0, unroll=2)

    @pl.when(i == 0)
    def _():
        start_tile(dest_ref, 0)

    @pl.when(i + 1 < pl.num_programs(0))
    def _():
        start_tile(dnext_ref, 1 - slot)

    pltpu.make_async_copy(buf_ref.at[slot], buf_ref.at[slot], sem.at[slot]).wait()

    f = _unpack_pair(_load_tokens(sh_ref, tm)).astype(F32)
    gate = gate_ref[...]
    for j in range(TOP_K):
        f = f + gate[:, j:j + 1] * _unpack_pair(_load_tokens(buf_ref.at[slot, j], tm)).astype(F32)
    o_ref[...] = _layer_norm(DEEPNORM_ALPHA * h_ref[...] + gf_ref[0] * f, g_ref[...], b_ref[...])


def _combine(ys, dest, gate, shared, h, mod3, ln_g, ln_b, dims):
    n, d = h.shape
    tm = COMBINE_TILE
    seq = _seq_of_tile(dims, tm)
    dest3 = dest.reshape(n // tm, 1, tm * TOP_K)
    return pl.pallas_call(
        _combine_kernel,
        out_shape=jax.ShapeDtypeStruct((n, d), F32),
        grid=(n // tm,),
        in_specs=[
            pl.BlockSpec((1, 1, tm * TOP_K), lambda i: (i, 0, 0), memory_space=pltpu.SMEM),
            pl.BlockSpec((1, 1, tm * TOP_K), lambda i: (jnp.minimum(i + 1, n // tm - 1), 0, 0),
                         memory_space=pltpu.SMEM),
            pl.BlockSpec((tm, LANES), lambda i: (i, 0)),
            pl.BlockSpec((tm * TOKEN_ROWS, LANES), lambda i: (i, 0)),
            pl.BlockSpec((tm, d), lambda i: (i, 0)),
            pl.BlockSpec((1, 1, d), lambda i: (seq(i) * 6 + 5, 0, 0)),
            pl.BlockSpec((1, d), lambda i: (0, 0)),
            pl.BlockSpec((1, d), lambda i: (0, 0)),
            pl.BlockSpec(memory_space=pl.ANY),
        ],
        out_specs=pl.BlockSpec((tm, d), lambda i: (i, 0)),
        scratch_shapes=[pltpu.VMEM((2, TOP_K, tm * TOKEN_ROWS, LANES), U32), pltpu.SemaphoreType.DMA((2,))],
        compiler_params=_cparams("arbitrary"),
        name="moe_combine_ln",
    )(dest3, dest3, gate, shared, h, mod3, ln_g.reshape(1, d), ln_b.reshape(1, d), ys)


def _moe(vfp, h, mod3, router_w, bias, w1, w3, w2, sw1, sw3, sw2, layer, ln_g, ln_b, dims, xs_buf):
    n = vfp.shape[0] // TOKEN_ROWS
    tmb = MOE_BLOCK
    idx, gate, rank, cnt = _router(vfp, router_w, bias)
    counts = cnt[0, :N_EXPERTS]
    padded = (counts + tmb - 1) // tmb * tmb
    pend = jnp.cumsum(padded)
    pstart = pend - padded
    n_blocks = (n * TOP_K + N_EXPERTS * (tmb - 1) + tmb - 1) // tmb
    blk_pos = jnp.arange(n_blocks, dtype=jnp.int32) * tmb
    blk_e = jnp.minimum(jnp.sum((pend[None, :] <= blk_pos[:, None]).astype(jnp.int32), axis=1), N_EXPERTS - 1)
    blk_on = (blk_pos < pend[-1]).astype(jnp.int32)
    e_sel = idx[:, :TOP_K]
    onehot = e_sel[:, :, None] == jnp.arange(N_EXPERTS, dtype=jnp.int32)[None, None, :]
    dest = rank[:, :TOP_K] + jnp.sum(jnp.where(onehot, pstart[None, None, :], 0), axis=-1)
    dest = dest.reshape(-1).astype(jnp.int32)
    if xs_buf is None:
        xs_buf = jnp.zeros((n_blocks * tmb * TOKEN_ROWS, LANES), U32)
    xs = _dispatch(vfp, dest, xs_buf)
    flat = lambda w: w.reshape((-1,) + w.shape[-2:])
    ys = _grouped_ffn(xs, blk_e + layer * N_EXPERTS, blk_on, flat(w1), flat(w3), flat(w2), tmb)
    ones = jnp.ones((n // tmb,), jnp.int32)
    shared = _grouped_ffn(vfp, jnp.full((n // tmb,), layer, jnp.int32), ones, sw1, sw3, sw2, tmb)
    return _combine(ys, dest, gate, shared, h, mod3, ln_g, ln_b, dims), xs


def _to_scan(rows, b):
    t = rows.shape[0] // b
    return rows.reshape(b, t, RW_HEADS, RW_HEAD).transpose(1, 3, 0, 2).reshape(t, RW_HEAD, b * RW_HEADS)


def _from_scan(y, b):
    t = y.shape[0]
    return y.reshape(t, RW_HEAD, b, RW_HEADS).transpose(2, 0, 3, 1).reshape(b * t, RW_WIDTH)


def _even_params(j, even_w_in, rw_mu, rw_w0, rw_w_up, rw_a0, rw_a_up, rw_g_up, rw_kk, rw_ka, rw_rk, rw_gn_g,
                 rw_gn_b, cv_w, cv_b, cv_ln_g, cv_ln_b):
    w3 = 3 * RW_WIDTH
    o_g, o_w, o_a = w3, w3 + GATE_LORA, w3 + GATE_LORA + 2 * DECAY_LORA
    o_cv = o_a + 2 * ICLR_LORA

    def relay(m):
        lead = m.shape[0]
        z = lambda k: jnp.zeros((lead, k), m.dtype)
        pieces = [m[:, :w3], m[:, o_g:o_g + GATE_LORA], z(LANES - GATE_LORA)]
        for d in range(2):
            pieces += [m[:, o_w + d * DECAY_LORA:o_w + (d + 1) * DECAY_LORA], z(LANES - DECAY_LORA)]
        for d in range(2):
            pieces += [m[:, o_a + d * ICLR_LORA:o_a + (d + 1) * ICLR_LORA], z(LANES - ICLR_LORA)]
        return jnp.concatenate(pieces, axis=1)

    w_in = even_w_in[j]
    w_pad = jnp.concatenate([relay(w_in), jnp.zeros((w_in.shape[0], 2 * 2048 - RW_STREAM_PAD), w_in.dtype),
                             w_in[:, o_cv:]], axis=1).astype(BF16)
    pad_rows = lambda m, k: jnp.concatenate([m, jnp.zeros((k - m.shape[0], m.shape[1]), m.dtype)], axis=0)
    head = jnp.arange(RW_WIDTH) // RW_HEAD
    vec = lambda m: m.reshape(1, -1)
    return {
        "w_in": w_pad,
        "mu": relay(rw_mu[j]),
        "w0": rw_w0[j], "a0": rw_a0[j],
        "w_up": jnp.stack([pad_rows(rw_w_up[j, d], LANES) for d in range(2)]).astype(BF16),
        "a_up": jnp.stack([pad_rows(rw_a_up[j, d], LANES) for d in range(2)]).astype(BF16),
        "g_up": pad_rows(rw_g_up[j], LANES).astype(BF16),
        "kk": vec(rw_kk[j]), "ka": vec(rw_ka[j]), "rk": vec(rw_rk[j]),
        "hs": (head[:, None] == head[None, :]).astype(BF16),
        "gn_g": vec(rw_gn_g[j]), "gn_b": vec(rw_gn_b[j]),
        "cv_w": pad_rows(cv_w[j], 32), "cv_b": vec(cv_b[j]), "cv_g": vec(cv_ln_g[j]), "cv_beta": vec(cv_ln_b[j]),
    }


def _even_mixer(h, mod3, prm, dims):
    p = _inproj(h, mod3, prm["w_in"], dims, 1, 0)
    r, v, kk, g, bon, dec, bvec, kd = _rwkv_features(p, prm, dims)
    b = dims[0]
    nl = b * dims[1]
    state = [jnp.zeros((RW_HEAD, RW_HEAD, LANES), F32)] * 2
    y_parts = []
    for rows in (slice(nl, None), slice(0, nl)):
        shared = [_to_scan(x[rows], b) for x in (r, kk, v)]
        y_sum = None
        for d in range(2):
            per_dir = [_to_scan(x[d, rows], b) for x in (dec, bvec, kd)]
            y_sum, state[d] = _wkv_scan(*shared, *per_dir, state[d], reverse=(d == 1), y_prev=y_sum)
        y_parts.append(_from_scan(y_sum, b))
    rw = _rwkv_post(jnp.concatenate([y_parts[1], y_parts[0]], axis=0), bon, g, prm)
    cv = _conformer_conv(p, prm, dims)
    return [rw, cv]


def _attn_mixer(h, mod3, w_in, qn, kn, tables, dims, with_ctx):
    p = _inproj(h, mod3, w_in, dims, 1, 0)
    q, k, v = _qkv_post(p, tables[0], tables[1], qn, kn, dims)
    out = _attention(q, k, v, dims, ctx_queries=False)
    if not with_ctx:
        return [out]
    return [jnp.concatenate([out, _attention(q, k, v, dims, ctx_queries=True)], axis=0)]


def kernel(x, c, ctx, c_ctx, ada_w, ada_b, ln1_g, ln1_b, ln2_g, ln2_b, even_w_in, even_w_out, rw_mu, rw_w0, rw_w_up, rw_a0, rw_a_up, rw_g_up, rw_kk, rw_ka, rw_rk, rw_gn_g, rw_gn_b, cv_w, cv_b, cv_ln_g, cv_ln_b, odd_w_in, odd_w_out, q_norm, k_norm, moe_router, moe_bias, moe_w1, moe_w3, moe_w2, sh_w1, sh_w3, sh_w2):
    bsz, s_len, d = x.shape
    ctx_len = ctx.shape[1]
    dims = (bsz, s_len, ctx_len)
    assert bsz * RW_HEADS == LANES, "the WKV scan maps (batch, head) pairs onto the 128 lanes"
    n_lat = bsz * s_len
    depth = ada_w.shape[0]

    c16 = jnp.zeros((16, d), F32).at[:bsz].set(c).at[bsz].set(c_ctx)
    mod_all = _ada_all(c16, ada_w, ada_b)[:, :bsz + 1].reshape(depth, (bsz + 1) * 6, 1, d)
    tables = _rope_tables(s_len, min(ROW_TILE, ctx_len))

    h = jnp.concatenate([x.reshape(n_lat, d), ctx.reshape(bsz * ctx_len, d)], axis=0)
    xs_buf = None
    for layer in range(depth):
        j = layer // 2
        last = layer == depth - 1
        mod3 = mod_all[layer]
        if layer % 2 == 0:
            prm = _even_params(j, even_w_in, rw_mu, rw_w0, rw_w_up, rw_a0, rw_a_up, rw_g_up, rw_kk, rw_ka, rw_rk,
                               rw_gn_g, rw_gn_b, cv_w, cv_b, cv_ln_g, cv_ln_b)
            acts = _even_mixer(h, mod3, prm, dims)
            w_out = even_w_out[j].astype(BF16)
        else:
            acts = _attn_mixer(h, mod3, odd_w_in[j].astype(BF16), q_norm[j], k_norm[j], tables, dims,
                               with_ctx=not last)
            w_out = odd_w_out[j].astype(BF16)
        h, vfp = _outproj(acts, w_out, h, mod3, ln1_g[layer], ln1_b[layer], dims)
        h, xs_buf = _moe(vfp, h, mod3, moe_router[layer], moe_bias[layer], moe_w1, moe_w3, moe_w2, sh_w1, sh_w3,
                         sh_w2, layer, ln2_g[layer], ln2_b[layer], dims, xs_buf)
    return h[:n_lat].reshape(bsz, s_len, d)
```

```python
import functools
import math

import jax
import jax.numpy as jnp
from jax import lax
from jax.experimental import pallas as pl
from jax.experimental.pallas import tpu as pltpu

F32 = jnp.float32
BF16 = jnp.bfloat16
U32 = jnp.uint32

D_MODEL = 2048
DEPTH = 4
GRID_W = 64

RW_HEADS = 16
RW_HEAD = 64
RW_WIDTH = RW_HEADS * RW_HEAD
DECAY_LORA = 96
ICLR_LORA = 96
GATE_LORA = 64
RW_GN_EPS = 64e-5
CONV_WIDTH = 1024
CONV_K = 31

ATT_HEADS = 16
ATT_KV_HEADS = 4
ATT_HEAD = 128
ATT_GROUP = ATT_HEADS // ATT_KV_HEADS
ATT_Q = ATT_HEADS * ATT_HEAD
ATT_KV = ATT_KV_HEADS * ATT_HEAD
ROPE_THETA = 10000.0
ROPE_PAIRS = ATT_HEAD // 4
QK_EPS = 1e-6

N_EXPERTS = 64
TOP_K = 6
EXPERT_FF = 384
ROUTE_SCALE = 2.5

DEEPNORM_ALPHA = (2 * DEPTH) ** 0.25
LN_EPS = 1e-5

LANES = 128
SMALL_W = 640
RW_STREAM_PAD = 3 * RW_WIDTH + SMALL_W
EVEN_IN_PAD = 6144
CONV_COL_BLOCK = 2
EXP_M05 = math.exp(-0.5)
NEG_BIG = -1e30
VMEM_LIMIT = 56 * 1024 * 1024

ROW_TILE = 256
HALO = 16
SCAN_STEPS = 64
MOE_BLOCK = 512
ROUTER_TILE = 512
DISPATCH_TILE = 256
COMBINE_TILE = 256
ATTN_ROW_BLOCK = 128


def _cparams(*sem):
    return pltpu.CompilerParams(dimension_semantics=tuple(sem), vmem_limit_bytes=VMEM_LIMIT)


def _layer_norm(x, g, b):
    mu = jnp.mean(x, axis=-1, keepdims=True)
    xc = x - mu
    var = jnp.mean(xc * xc, axis=-1, keepdims=True)
    return xc * lax.rsqrt(var + LN_EPS) * g + b


def _sigmoid(x):
    return 1.0 / (1.0 + jnp.exp(-x))


def _pack_pair(x):
    k = x.shape[1] // 2
    lo = lax.bitcast_convert_type(x[:, :k].astype(BF16).astype(F32), U32) >> 16
    hi = lax.bitcast_convert_type(x[:, k:].astype(BF16).astype(F32), U32) & jnp.uint32(0xFFFF0000)
    return hi | lo


def _unpack_pair(w):
    lo = lax.bitcast_convert_type(w << 16, F32)
    hi = lax.bitcast_convert_type(w & jnp.uint32(0xFFFF0000), F32)
    return jnp.concatenate([lo, hi], axis=1).astype(BF16)


TOKEN_ROWS = 8


def _load_tokens(ref, n_tok):
    return jnp.concatenate([ref[pl.ds(s, n_tok, stride=TOKEN_ROWS), :] for s in range(TOKEN_ROWS)], axis=1)


def _store_tokens(ref, words):
    n_tok = words.shape[0]
    for s in range(TOKEN_ROWS):
        ref[pl.ds(s, n_tok, stride=TOKEN_ROWS), :] = words[:, s * LANES:(s + 1) * LANES]


def _col_tile(n, cap=1280):
    best = LANES
    for t in range(LANES, cap + 1, LANES):
        if n % t == 0:
            best = t
    return best


def _ada_kernel(c_ref, w_ref, b_ref, o_ref):
    c = c_ref[...]
    s = (c * _sigmoid(c)).astype(BF16)
    o_ref[0] = jnp.dot(s, w_ref[0].astype(BF16), preferred_element_type=F32) + b_ref[0]


def _ada_all(c16, ada_w, ada_b):
    depth, d, n6 = ada_w.shape
    tn = 1024
    return pl.pallas_call(
        _ada_kernel,
        out_shape=jax.ShapeDtypeStruct((depth, 16, n6), F32),
        grid=(depth, n6 // tn),
        in_specs=[
            pl.BlockSpec((16, d), lambda l, j: (0, 0)),
            pl.BlockSpec((1, d, tn), lambda l, j: (l, 0, j)),
            pl.BlockSpec((1, 1, tn), lambda l, j: (l, 0, j)),
        ],
        out_specs=pl.BlockSpec((1, 16, tn), lambda l, j: (l, 0, j)),
        compiler_params=_cparams("parallel", "parallel"),
        name="ada_mod",
    )(c16, ada_w, ada_b.reshape(depth, 1, n6))


def _inproj_kernel(x_ref, sc_ref, sh_ref, w_ref, o_ref, u_ref):
    @pl.when(pl.program_id(1) == 0)
    def _():
        u_ref[...] = (x_ref[...] * (1.0 + sc_ref[0]) + sh_ref[0]).astype(BF16)

    o_ref[...] = jnp.dot(u_ref[...], w_ref[...], preferred_element_type=F32).astype(o_ref.dtype)


def _inproj(h, mod3, w, dims, sc_idx, sh_idx):
    n, d = h.shape
    nout = w.shape[1]
    tm = min(512, dims[1])
    tn = _col_tile(nout)
    seq = _seq_of_tile(dims, tm)
    return pl.pallas_call(
        _inproj_kernel,
        out_shape=jax.ShapeDtypeStruct((n, nout), BF16),
        grid=(n // tm, nout // tn),
        in_specs=[
            pl.BlockSpec((tm, d), lambda i, j: (i, 0)),
            pl.BlockSpec((1, 1, d), lambda i, j: (seq(i) * 6 + sc_idx, 0, 0)),
            pl.BlockSpec((1, 1, d), lambda i, j: (seq(i) * 6 + sh_idx, 0, 0)),
            pl.BlockSpec((d, tn), lambda i, j: (0, j)),
        ],
        out_specs=pl.BlockSpec((tm, tn), lambda i, j: (i, j)),
        scratch_shapes=[pltpu.VMEM((tm, d), BF16)],
        compiler_params=_cparams("parallel", "arbitrary"),
        name="inproj",
    )(h, mod3, mod3, w)


def _seq_of_tile(dims, tm):
    b, s, _ = dims

    def seq(i):
        return jnp.minimum((i * tm) // s, b)

    return seq


def _outproj_kernel(*refs, splits):
    n_in = len(splits)
    a_refs = refs[:n_in]
    w_ref, h_ref, ga_ref, scf_ref, shf_ref, g_ref, b_ref, ho_ref, vf_ref = refs[n_in:]
    acc = None
    off = 0
    for a_ref, k in zip(a_refs, splits):
        part = jnp.dot(a_ref[...], w_ref[off:off + k, :], preferred_element_type=F32)
        acc = part if acc is None else acc + part
        off += k
    hn = _layer_norm(DEEPNORM_ALPHA * h_ref[...] + ga_ref[0] * acc, g_ref[...], b_ref[...])
    ho_ref[...] = hn
    _store_tokens(vf_ref, _pack_pair(hn * (1.0 + scf_ref[0]) + shf_ref[0]))


def _outproj(acts, w, h, mod3, ln_g, ln_b, dims):
    n, d = acts[0].shape[0], h.shape[1]
    tm = min(256, dims[1])
    seq = _seq_of_tile(dims, tm)
    splits = tuple(a.shape[1] for a in acts)
    in_specs = [pl.BlockSpec((tm, k), lambda i: (i, 0)) for k in splits]
    in_specs += [
        pl.BlockSpec(w.shape, lambda i: (0, 0)),
        pl.BlockSpec((tm, d), lambda i: (i, 0)),
        pl.BlockSpec((1, 1, d), lambda i: (seq(i) * 6 + 2, 0, 0)),
        pl.BlockSpec((1, 1, d), lambda i: (seq(i) * 6 + 4, 0, 0)),
        pl.BlockSpec((1, 1, d), lambda i: (seq(i) * 6 + 3, 0, 0)),
        pl.BlockSpec((1, d), lambda i: (0, 0)),
        pl.BlockSpec((1, d), lambda i: (0, 0)),
    ]
    return pl.pallas_call(
        functools.partial(_outproj_kernel, splits=splits),
        out_shape=(jax.ShapeDtypeStruct((n, d), F32), jax.ShapeDtypeStruct((n * TOKEN_ROWS, LANES), U32)),
        grid=(n // tm,),
        in_specs=in_specs,
        out_specs=(pl.BlockSpec((tm, d), lambda i: (i, 0)), pl.BlockSpec((tm * TOKEN_ROWS, LANES), lambda i: (i, 0))),
        compiler_params=_cparams("parallel"),
        name="outproj_ln",
    )(*acts, w, h, mod3, mod3, mod3, ln_g.reshape(1, d), ln_b.reshape(1, d))


def _tile_flags(dims, tb):
    b, s, ctx = dims
    n = b * (s + ctx)
    nlt = (b * s) // tb
    nbl = s // tb
    nbc = ctx // tb

    def first_last(i):
        il = i % nbl
        ic = (i - nlt) % nbc
        lat = i < nlt
        first = jnp.where(lat, il == 0, ic == 0)
        last = jnp.where(lat, il == nbl - 1, ic == nbc - 1)
        return first, last

    def prev_blk(i):
        return jnp.maximum((i * tb) // HALO - 1, 0)

    def next_blk(i):
        return jnp.minimum(((i + 1) * tb) // HALO, n // HALO - 1)

    return first_last, prev_blk, next_blk


def _head_sum(x, hs_ref):
    hi = x.astype(BF16)
    lo = (x - hi.astype(F32)).astype(BF16)
    hs = hs_ref[...]
    return jnp.dot(hi, hs, preferred_element_type=F32) + jnp.dot(lo, hs, preferred_element_type=F32)


def _feat_kernel(p_ref, pp_ref, pn_ref, mu_ref, w0_ref, a0_ref, wup_ref, aup_ref, gup_ref, kk_ref, ka_ref,
                 rk_ref, hs_ref, r_o, v_o, kkn_o, g_o, bon_o, dec_o, b_o, kd_o, buf_ref, *, tb, first_last):
    first, last = first_last(pl.program_id(0))
    w = RW_WIDTH
    buf_ref[8:8 + tb, :] = p_ref[...].astype(F32)
    buf_ref[7:8, :] = jnp.where(first, 0.0, pp_ref[HALO - 1:HALO, :].astype(F32))
    buf_ref[8 + tb:9 + tb, :] = jnp.where(last, 0.0, pn_ref[0:1, :].astype(F32))

    def shifted(c0, c1):
        cur = buf_ref[8:8 + tb, c0:c1]
        prev = buf_ref[7:7 + tb, c0:c1]
        nxt = buf_ref[9:9 + tb, c0:c1]
        return cur + mu_ref[0:1, c0:c1] * (prev - cur) + mu_ref[1:2, c0:c1] * (nxt - cur)

    r = shifted(0, w)
    k = shifted(w, 2 * w)
    v = shifted(2 * w, 3 * w)
    sm = shifted(3 * w, 3 * w + SMALL_W)

    g = jnp.dot(_sigmoid(sm[:, 0:LANES]).astype(BF16), gup_ref[...], preferred_element_type=F32)
    kkr = k * kk_ref[...]
    kk = kkr * lax.rsqrt(_head_sum(kkr * kkr, hs_ref) + 1e-12)

    ksum = None
    for d in range(2):
        wd = sm[:, LANES * (1 + d):LANES * (2 + d)]
        ad = sm[:, LANES * (3 + d):LANES * (4 + d)]
        xw = w0_ref[d:d + 1, :] + jnp.dot(jnp.tanh(wd).astype(BF16), wup_ref[d], preferred_element_type=F32)
        dec_o[d] = jnp.exp(-EXP_M05 * _sigmoid(xw))
        a = _sigmoid(a0_ref[d:d + 1, :] + jnp.dot(ad.astype(BF16), aup_ref[d], preferred_element_type=F32))
        b_o[d] = (a * kk).astype(BF16)
        kd = k * (1.0 + (a - 1.0) * ka_ref[...])
        kd_o[d] = kd.astype(BF16)
        ksum = kd if ksum is None else ksum + kd

    bon_o[...] = (_head_sum(r * ksum * rk_ref[...], hs_ref) * v).astype(BF16)
    r_o[...] = r.astype(BF16)
    v_o[...] = v.astype(BF16)
    kkn_o[...] = kk.astype(BF16)
    g_o[...] = g.astype(BF16)


def _rwkv_features(p, prm, dims):
    n = p.shape[0]
    tb = min(ROW_TILE, dims[2])
    first_last, prev_blk, next_blk = _tile_flags(dims, tb)
    w = RW_WIDTH
    pw = RW_STREAM_PAD
    full2 = lambda shape: pl.BlockSpec(shape, lambda i: (0, 0))
    full3 = lambda shape: pl.BlockSpec(shape, lambda i: (0, 0, 0))
    row = pl.BlockSpec((tb, w), lambda i: (i, 0))
    row2 = pl.BlockSpec((2, tb, w), lambda i: (0, i, 0))
    sd = lambda dt: jax.ShapeDtypeStruct((n, w), dt)
    sd2 = lambda dt: jax.ShapeDtypeStruct((2, n, w), dt)
    return pl.pallas_call(
        functools.partial(_feat_kernel, tb=tb, first_last=first_last),
        out_shape=(sd(BF16), sd(BF16), sd(BF16), sd(BF16), sd(BF16), sd2(F32), sd2(BF16), sd2(BF16)),
        grid=(n // tb,),
        in_specs=[
            pl.BlockSpec((tb, pw), lambda i: (i, 0)),
            pl.BlockSpec((HALO, pw), lambda i: (prev_blk(i), 0)),
            pl.BlockSpec((HALO, pw), lambda i: (next_blk(i), 0)),
            full2((2, pw)), full2((2, w)), full2((2, w)),
            full3((2, LANES, w)), full3((2, LANES, w)), full2((LANES, w)),
            full2((1, w)), full2((1, w)), full2((1, w)), full2((w, w)),
        ],
        out_specs=(row, row, row, row, row, row2, row2, row2),
        scratch_shapes=[pltpu.VMEM((tb + 16, pw), F32)],
        compiler_params=_cparams("parallel"),
        name="rwkv_features",
    )(p, p, p, prm["mu"], prm["w0"], prm["a0"], prm["w_up"], prm["a_up"], prm["g_up"], prm["kk"], prm["ka"],
      prm["rk"], prm["hs"])


def _wkv_kernel(r_ref, kk_ref, v_ref, w_ref, b_ref, kd_ref, s0_ref, *refs, steps, reverse, add_prev):
    yp_ref = refs[0] if add_prev else None
    y_ref, st_ref, s_ref, rf, kkf, vf, bf, kdf = refs[1:] if add_prev else refs
    i = pl.program_id(0)

    @pl.when(i == 0)
    def _():
        s_ref[...] = s0_ref[...]

    for src, dst in ((r_ref, rf), (kk_ref, kkf), (v_ref, vf), (b_ref, bf), (kd_ref, kdf)):
        dst[...] = src[...].astype(F32)

    nk = RW_HEAD
    half = nk // 2
    t_first = steps - 1 if reverse else 0

    def s_dot_kk(t, lo):
        acc = jnp.zeros((half, LANES), F32)
        for k in range(nk):
            acc = acc + s_ref[k, lo:lo + half, :] * kkf[t, pl.ds(k, 1), :]
        return acc

    def step(j, sks):
        t = steps - 1 - j if reverse else j
        t_next = jnp.maximum(t - 1, 0) if reverse else jnp.minimum(t + 1, steps - 1)
        sks_next = []
        for sk, lo in zip(sks, (0, half)):
            vt = vf[t, lo:lo + half, :]
            y = jnp.zeros((half, LANES), F32)
            sk_next = jnp.zeros((half, LANES), F32)
            for k in range(nk):
                s = (s_ref[k, lo:lo + half, :] * w_ref[t, pl.ds(k, 1), :] - sk * bf[t, pl.ds(k, 1), :]
                     + vt * kdf[t, pl.ds(k, 1), :])
                s_ref[k, lo:lo + half, :] = s
                y = y + s * rf[t, pl.ds(k, 1), :]
                sk_next = sk_next + s * kkf[t_next, pl.ds(k, 1), :]
            y_ref[t, lo:lo + half, :] = y + yp_ref[t, lo:lo + half, :] if add_prev else y
            sks_next.append(sk_next)
        return tuple(sks_next)

    lax.fori_loop(0, steps, step, (s_dot_kk(t_first, 0), s_dot_kk(t_first, half)))

    @pl.when(i == pl.num_programs(0) - 1)
    def _():
        st_ref[...] = s_ref[...]


def _wkv_scan(r, kk, v, w, b, kd, s0, reverse, y_prev=None):
    t_all = r.shape[0]
    extra = [] if y_prev is None else [y_prev]
    steps = min(SCAN_STEPS, t_all)
    nblk = t_all // steps
    blk = (lambda i: (nblk - 1 - i, 0, 0)) if reverse else (lambda i: (i, 0, 0))
    seq = pl.BlockSpec((steps, RW_HEAD, LANES), blk)
    state = pl.BlockSpec((RW_HEAD, RW_HEAD, LANES), lambda i: (0, 0, 0))
    buf = pltpu.VMEM((steps, RW_HEAD, LANES), F32)
    return pl.pallas_call(
        functools.partial(_wkv_kernel, steps=steps, reverse=reverse, add_prev=y_prev is not None),
        out_shape=(jax.ShapeDtypeStruct((t_all, RW_HEAD, LANES), F32),
                   jax.ShapeDtypeStruct((RW_HEAD, RW_HEAD, LANES), F32)),
        grid=(nblk,),
        in_specs=[seq, seq, seq, seq, seq, seq, state] + [seq] * len(extra),
        out_specs=(seq, state),
        scratch_shapes=[pltpu.VMEM((RW_HEAD, RW_HEAD, LANES), F32), buf, buf, buf, buf, buf],
        compiler_params=_cparams("arbitrary"),
        name="wkv_scan_bwd" if reverse else "wkv_scan_fwd",
    )(r, kk, v, w, b, kd, s0, *extra)


def _rwpost_kernel(y_ref, bon_ref, g_ref, gng_ref, gnb_ref, hs_ref, o_ref):
    y = y_ref[...]
    inv = 1.0 / RW_HEAD
    yc = y - _head_sum(y, hs_ref) * inv
    var = _head_sum(yc * yc, hs_ref) * inv
    yn = yc * lax.rsqrt(var + RW_GN_EPS) * gng_ref[...] + gnb_ref[...]
    o_ref[...] = ((yn + bon_ref[...].astype(F32)) * g_ref[...].astype(F32)).astype(BF16)


def _rwkv_post(y, bon, g, prm):
    n, w = y.shape
    tb = 256
    row = pl.BlockSpec((tb, w), lambda i: (i, 0))
    vec = pl.BlockSpec((1, w), lambda i: (0, 0))
    return pl.pallas_call(
        _rwpost_kernel,
        out_shape=jax.ShapeDtypeStruct((n, w), BF16),
        grid=(n // tb,),
        in_specs=[row, row, row, vec, vec, pl.BlockSpec((w, w), lambda i: (0, 0))],
        out_specs=row,
        compiler_params=_cparams("parallel"),
        name="rwkv_post",
    )(y, bon, g, prm["gn_g"], prm["gn_b"], prm["hs"])


def _conv_kernel(p_ref, pp_ref, pn_ref, w_ref, b_ref, g_ref, beta_ref, o_ref, z_ref, *, tb, first_last):
    first, last = first_last(pl.program_id(0))
    c = CONV_WIDTH

    def glu(x):
        x = x.astype(F32)
        return x[:, :c] * _sigmoid(x[:, c:])

    z_ref[HALO:HALO + tb, :] = glu(p_ref[...])
    z_ref[0:HALO, :] = jnp.where(first, 0.0, glu(pp_ref[...]))
    z_ref[HALO + tb:2 * HALO + tb, :] = jnp.where(last, 0.0, glu(pn_ref[...]))
    acc = jnp.zeros((tb, c), F32)
    base = HALO - CONV_K // 2
    for j in range(CONV_K):
        acc = acc + z_ref[base + j:base + j + tb, :] * w_ref[j:j + 1, :]
    zn = _layer_norm(acc + b_ref[...], g_ref[...], beta_ref[...])
    o_ref[...] = (zn * _sigmoid(zn)).astype(BF16)


def _conformer_conv(p, prm, dims):
    n = p.shape[0]
    tb = min(ROW_TILE, dims[2])
    first_last, prev_blk, next_blk = _tile_flags(dims, tb)
    c = CONV_WIDTH
    cb = CONV_COL_BLOCK
    vec = pl.BlockSpec((1, c), lambda i: (0, 0))
    return pl.pallas_call(
        functools.partial(_conv_kernel, tb=tb, first_last=first_last),
        out_shape=jax.ShapeDtypeStruct((n, c), BF16),
        grid=(n // tb,),
        in_specs=[
            pl.BlockSpec((tb, 2 * c), lambda i: (i, cb)),
            pl.BlockSpec((HALO, 2 * c), lambda i: (prev_blk(i), cb)),
            pl.BlockSpec((HALO, 2 * c), lambda i: (next_blk(i), cb)),
            pl.BlockSpec((32, c), lambda i: (0, 0)), vec, vec, vec,
        ],
        out_specs=pl.BlockSpec((tb, c), lambda i: (i, 0)),
        scratch_shapes=[pltpu.VMEM((tb + 2 * HALO, c), F32)],
        compiler_params=_cparams("parallel"),
        name="conformer_conv",
    )(p, p, p, prm["cv_w"], prm["cv_b"], prm["cv_g"], prm["cv_beta"])


def _qkv_kernel(p_ref, cos_ref, sin_ref, qn_ref, kn_ref, perm_ref, ones_ref, q_o, k_o, v_o):
    cosv = cos_ref[...]
    sinv = sin_ref[...]

    def times_01(x, w_ref):
        hi = x.astype(BF16)
        lo = (x - hi.astype(F32)).astype(BF16)
        w = w_ref[...]
        return jnp.dot(hi, w, preferred_element_type=F32) + jnp.dot(lo, w, preferred_element_type=F32)

    def norm_rope(x, gain, scale):
        x = x.astype(F32)
        mean_sq = times_01(x * x, ones_ref) * (1.0 / ATT_HEAD)
        xn = x * lax.rsqrt(mean_sq + QK_EPS) * gain
        swapped = times_01(xn, perm_ref)
        return (xn * cosv + swapped * sinv) * scale

    hd = ATT_HEAD
    ones = jnp.ones((p_ref.shape[0], hd), BF16)
    for h in range(ATT_HEADS):
        q_o[h] = norm_rope(p_ref[:, h * hd:(h + 1) * hd], qn_ref[...], 1.0 / math.sqrt(ATT_HEAD)).astype(BF16)
    for h in range(ATT_KV_HEADS):
        k_o[h] = norm_rope(p_ref[:, ATT_Q + h * hd:ATT_Q + (h + 1) * hd], kn_ref[...], 1.0).T.astype(BF16)
        v_o[h, :, :hd] = p_ref[:, ATT_Q + ATT_KV + h * hd:ATT_Q + ATT_KV + (h + 1) * hd]
        v_o[h, :, hd:] = ones


def _qkv_post(p, cos_t, sin_t, qn, kn, dims):
    b, s, ctx = dims
    n = p.shape[0]
    tb = min(ROW_TILE, ctx)
    nlt = (b * s) // tb
    nbl = s // tb
    hd = ATT_HEAD

    def tab(i):
        return jnp.where(i < nlt, i % nbl, nbl)

    lane = jnp.arange(hd)
    partner = jnp.where(lane % 64 < 32, lane + 32, lane - 32)
    perm = (lane[:, None] == partner[None, :]).astype(BF16)

    return pl.pallas_call(
        _qkv_kernel,
        out_shape=(jax.ShapeDtypeStruct((ATT_HEADS, n, hd), BF16),
                   jax.ShapeDtypeStruct((ATT_KV_HEADS, hd, n), BF16),
                   jax.ShapeDtypeStruct((ATT_KV_HEADS, n, 2 * hd), BF16)),
        grid=(n // tb,),
        in_specs=[
            pl.BlockSpec((tb, p.shape[1]), lambda i: (i, 0)),
            pl.BlockSpec((tb, hd), lambda i: (tab(i), 0)),
            pl.BlockSpec((tb, hd), lambda i: (tab(i), 0)),
            pl.BlockSpec((1, hd), lambda i: (0, 0)),
            pl.BlockSpec((1, hd), lambda i: (0, 0)),
            pl.BlockSpec((hd, hd), lambda i: (0, 0)),
            pl.BlockSpec((hd, hd), lambda i: (0, 0)),
        ],
        out_specs=(pl.BlockSpec((ATT_HEADS, tb, hd), lambda i: (0, i, 0)),
                   pl.BlockSpec((ATT_KV_HEADS, hd, tb), lambda i: (0, 0, i)),
                   pl.BlockSpec((ATT_KV_HEADS, tb, 2 * hd), lambda i: (0, i, 0))),
        compiler_params=_cparams("parallel"),
        name="qkv_norm_rope",
    )(p, cos_t, sin_t, qn.reshape(1, hd), kn.reshape(1, hd), perm, jnp.ones((hd, hd), BF16))


def _rope_tables(s, pad_rows):
    rows_n = s // GRID_W
    row = jnp.repeat(jnp.arange(rows_n), GRID_W)
    col = jnp.tile(jnp.arange(GRID_W), rows_n)
    inv = ROPE_THETA ** (-jnp.arange(ROPE_PAIRS, dtype=F32) / ROPE_PAIRS)
    ang_r = row[:, None] * inv
    ang_c = col[:, None] * inv
    cos_t = jnp.concatenate([jnp.cos(ang_r), jnp.cos(ang_r), jnp.cos(ang_c), jnp.cos(ang_c)], axis=1)
    sin_t = jnp.concatenate([-jnp.sin(ang_r), jnp.sin(ang_r), -jnp.sin(ang_c), jnp.sin(ang_c)], axis=1)
    cos_t = jnp.concatenate([cos_t, jnp.ones((pad_rows, ATT_HEAD), F32)], axis=0)
    sin_t = jnp.concatenate([sin_t, jnp.zeros((pad_rows, ATT_HEAD), F32)], axis=0)
    return cos_t, sin_t


def _attn_kernel(q_ref, *refs, seg_lens, kc):
    n_seg = len(seg_lens)
    o_ref, s_ref, mx_ref, p_ref = refs[2 * n_seg:]
    g, tq, hd = q_ref.shape

    @pl.when(pl.program_id(0) == 0)
    def _():
        p_ref[...] = jnp.zeros_like(p_ref)

    acc = None
    off = 0
    for si, n_keys in enumerate(seg_lens):
        pv = jnp.dot(p_ref[:, off:off + n_keys], refs[2 * si + 1][0], preferred_element_type=F32)
        acc = pv if acc is None else acc + pv
        off += n_keys

    q = q_ref[...].reshape(g * tq, hd)
    chunks = []
    off = 0
    for si, n_keys in enumerate(seg_lens):
        for c0 in range(0, n_keys, kc):
            c = min(kc, n_keys - c0)
            chunks.append((si, c0, c, off))
            off += c
    for n, (si, c0, c, off) in enumerate(chunks):
        s = jnp.dot(q, refs[2 * si][0, :, c0:c0 + c], preferred_element_type=F32)
        s_ref[:, off:off + c] = s
        part = s[:, :LANES]
        for j in range(1, c // LANES):
            part = jnp.maximum(part, s[:, j * LANES:(j + 1) * LANES])
        mx_ref[...] = part if n == 0 else jnp.maximum(mx_ref[...], part)

    o = acc[:, :hd] / jnp.maximum(acc[:, hd:], 1e-30)
    for gi in range(g):
        o_ref[:, gi * hd:(gi + 1) * hd] = o[gi * tq:(gi + 1) * tq, :].astype(o_ref.dtype)

    mx_ref[...] = jnp.broadcast_to(jnp.max(mx_ref[...], axis=-1, keepdims=True), mx_ref.shape)
    rb = min(ATTN_ROW_BLOCK, g * tq)

    def exp_rows(i, carry):
        r0 = pl.multiple_of(i * rb, rb)
        mb = mx_ref[pl.ds(r0, rb), :]
        for _, _, c, off in chunks:
            e = jnp.exp(s_ref[pl.ds(r0, rb), off:off + c] - jnp.concatenate([mb] * (c // LANES), axis=1))
            p_ref[pl.ds(r0, rb), off:off + c] = e.astype(BF16)
        return carry

    lax.fori_loop(0, (g * tq) // rb, exp_rows, 0)


def _attention(q, kt, v1, dims, ctx_queries):
    b, s, ctx = dims
    hd = ATT_HEAD
    g = ATT_GROUP
    kvh = ATT_KV_HEADS
    nl = b * s
    if ctx_queries:
        tq, nq, rows, q_base = ctx, 1, b * ctx, nl // ctx
        segs = [(ctx, nl // ctx)]
    else:
        tq = min(256, s)
        nq, rows, q_base = s // tq, nl, 0
        segs = [(s, 0), (ctx, nl // ctx)]
    n_tiles = b * kvh * nq

    def split(tile):
        return tile // (kvh * nq), (tile // nq) % kvh, tile % nq

    cur = lambda t: split(jnp.minimum(t, n_tiles - 1))
    prev = lambda t: split(jnp.maximum(t - 1, 0))

    def q_map(t):
        bi, kv, qi = cur(t)
        return kv, q_base + bi * nq + qi, 0

    def o_map(t):
        bi, kv, qi = prev(t)
        return bi * nq + qi, kv

    in_specs = [pl.BlockSpec((g, tq, hd), q_map)]
    args = [q]
    for n_keys, base in segs:
        def k_map(t, base=base):
            bi, kv, _ = cur(t)
            return kv, 0, base + bi

        def v_map(t, base=base):
            bi, kv, _ = prev(t)
            return kv, base + bi, 0

        in_specs += [pl.BlockSpec((1, hd, n_keys), k_map), pl.BlockSpec((1, n_keys, 2 * hd), v_map)]
        args += [kt, v1]
    n_all = sum(n for n, _ in segs)
    return pl.pallas_call(
        functools.partial(_attn_kernel, seg_lens=tuple(n for n, _ in segs), kc=512),
        out_shape=jax.ShapeDtypeStruct((rows, ATT_Q), BF16),
        grid=(n_tiles + 1,),
        in_specs=in_specs,
        out_specs=pl.BlockSpec((tq, g * hd), o_map),
        scratch_shapes=[pltpu.VMEM((g * tq, n_all), F32), pltpu.VMEM((g * tq, LANES), F32),
                        pltpu.VMEM((g * tq, n_all), BF16)],
        compiler_params=_cparams("arbitrary"),
        name="gqa_attention_ctx" if ctx_queries else "gqa_attention",
    )(*args)


def _router_kernel(x_ref, w_ref, bias_ref, tri_ref, idx_o, gate_o, rank_o, cnt_o, carry_ref):
    @pl.when(pl.program_id(0) == 0)
    def _():
        carry_ref[...] = jnp.zeros_like(carry_ref)

    tm = x_ref.shape[0] // TOKEN_ROWS
    x = _unpack_pair(_load_tokens(x_ref, tm))
    scores = _sigmoid(jnp.dot(x, w_ref[...], preferred_element_type=F32))
    lane = lax.broadcasted_iota(jnp.int32, (tm, LANES), 1)
    sel = jnp.where(lane < N_EXPERTS, scores + bias_ref[...], NEG_BIG)
    picks = []
    gates = []
    chosen = jnp.zeros((tm, LANES), F32)
    for _ in range(TOP_K):
        mx = jnp.max(sel, axis=-1, keepdims=True)
        idx = jnp.min(jnp.where(sel == mx, lane, LANES), axis=-1, keepdims=True)
        hit = lane == idx
        picks.append((idx, hit))
        gates.append(jnp.sum(jnp.where(hit, scores, 0.0), axis=-1, keepdims=True))
        chosen = chosen + hit.astype(F32)
        sel = jnp.where(hit, NEG_BIG, sel)
    gsum = gates[0]
    for gj in gates[1:]:
        gsum = gsum + gj
    before = jnp.dot(tri_ref[...], chosen.astype(BF16), preferred_element_type=F32) + carry_ref[...]
    idx_out = jnp.zeros((tm, LANES), jnp.int32)
    gate_out = jnp.zeros((tm, LANES), F32)
    rank_out = jnp.zeros((tm, LANES), jnp.int32)
    for j, (idx, hit) in enumerate(picks):
        rank = jnp.sum(jnp.where(hit, before, 0.0), axis=-1, keepdims=True).astype(jnp.int32)
        idx_out = jnp.where(lane == j, idx, idx_out)
        gate_out = jnp.where(lane == j, ROUTE_SCALE * gates[j] / gsum, gate_out)
        rank_out = jnp.where(lane == j, rank, rank_out)
    idx_o[...] = idx_out
    gate_o[...] = gate_out
    rank_o[...] = rank_out
    carry_ref[...] = carry_ref[...] + jnp.sum(chosen, axis=0, keepdims=True)
    cnt_o[...] = carry_ref[...].astype(jnp.int32)


def _router(vfp, router_w, bias):
    n = vfp.shape[0] // TOKEN_ROWS
    tm = ROUTER_TILE if n % ROUTER_TILE == 0 else 128
    d = router_w.shape[0]
    wpad = jnp.zeros((d, LANES), BF16).at[:, :N_EXPERTS].set(router_w.astype(BF16))
    bpad = jnp.zeros((1, LANES), F32).at[0, :N_EXPERTS].set(bias)
    tri = (lax.broadcasted_iota(jnp.int32, (tm, tm), 1) < lax.broadcasted_iota(jnp.int32, (tm, tm), 0)).astype(BF16)
    row = pl.BlockSpec((tm, LANES), lambda i: (i, 0))
    return pl.pallas_call(
        _router_kernel,
        out_shape=(jax.ShapeDtypeStruct((n, LANES), jnp.int32), jax.ShapeDtypeStruct((n, LANES), F32),
                   jax.ShapeDtypeStruct((n, LANES), jnp.int32), jax.ShapeDtypeStruct((1, LANES), jnp.int32)),
        grid=(n // tm,),
        in_specs=[
            pl.BlockSpec((tm * TOKEN_ROWS, LANES), lambda i: (i, 0)),
            pl.BlockSpec((d, LANES), lambda i: (0, 0)),
            pl.BlockSpec((1, LANES), lambda i: (0, 0)),
            pl.BlockSpec((tm, tm), lambda i: (0, 0)),
        ],
        out_specs=(row, row, row, pl.BlockSpec((1, LANES), lambda i: (0, 0))),
        scratch_shapes=[pltpu.VMEM((1, LANES), F32)],
        compiler_params=_cparams("arbitrary"),
        name="moe_router",
    )(vfp, wpad, bpad, tri)


def _token_tile(ref, t):
    return ref.at[pl.ds(pl.multiple_of(t * TOKEN_ROWS, TOKEN_ROWS), TOKEN_ROWS)]


def _dispatch_kernel(dest_ref, x_ref, xs_in_ref, xs_ref, sem):
    del xs_in_ref
    tm = x_ref.shape[0] // TOKEN_ROWS

    def copy(r, d):
        return pltpu.make_async_copy(_token_tile(x_ref, r), _token_tile(xs_ref, d), sem)

    def start(r, carry):
        for j in range(TOP_K):
            copy(r, dest_ref[0, 0, r * TOP_K + j]).start(priority=j % 2)
        return carry

    lax.fori_loop(0, tm, start, 0, unroll=2)

    done = xs_ref.at[pl.ds(0, tm * TOP_K * TOKEN_ROWS)]
    pltpu.make_async_copy(done, done, sem).wait()


def _dispatch(vfp, dest, xs0):
    n = vfp.shape[0] // TOKEN_ROWS
    tm = DISPATCH_TILE if n % DISPATCH_TILE == 0 else 128
    dest3 = dest.reshape(n // tm, 1, tm * TOP_K)
    return pl.pallas_call(
        _dispatch_kernel,
        out_shape=jax.ShapeDtypeStruct(xs0.shape, U32),
        grid=(n // tm,),
        in_specs=[
            pl.BlockSpec((1, 1, tm * TOP_K), lambda i: (i, 0, 0), memory_space=pltpu.SMEM),
            pl.BlockSpec((tm * TOKEN_ROWS, LANES), lambda i: (i, 0)),
            pl.BlockSpec(memory_space=pl.ANY),
        ],
        out_specs=pl.BlockSpec(memory_space=pl.ANY),
        scratch_shapes=[pltpu.SemaphoreType.DMA(())],
        input_output_aliases={2: 0},
        compiler_params=_cparams("arbitrary"),
        name="moe_dispatch",
    )(dest3, vfp, xs0)


def _ffn_kernel(be_ref, on_ref, x_ref, w1_ref, w3_ref, w2_ref, o_ref, w13b, w2b):
    i = pl.program_id(0)
    prev = be_ref[jnp.maximum(i - 1, 0)]
    ff = w1_ref.shape[2]

    @pl.when((i == 0) | (be_ref[i] != prev))
    def _():
        w13b[:, :ff] = w1_ref[0].astype(BF16)
        w13b[:, ff:] = w3_ref[0].astype(BF16)
        w2b[...] = w2_ref[0].astype(BF16)

    @pl.when(on_ref[i] != 0)
    def _():
        x = _unpack_pair(_load_tokens(x_ref, x_ref.shape[0] // TOKEN_ROWS))
        ac = jnp.dot(x, w13b[...], preferred_element_type=F32)
        a = ac[:, :ff]
        hmid = (a * _sigmoid(a) * ac[:, ff:]).astype(BF16)
        _store_tokens(o_ref, _pack_pair(jnp.dot(hmid, w2b[...], preferred_element_type=F32)))

    @pl.when(on_ref[i] == 0)
    def _():
        o_ref[...] = jnp.zeros_like(o_ref)


def _grouped_ffn(xs, blk_e, blk_on, w1, w3, w2, tm):
    n_blocks = blk_e.shape[0]
    d, ff = w1.shape[1], w1.shape[2]
    tile = pl.BlockSpec((tm * TOKEN_ROWS, LANES), lambda i, be, on: (i, 0))
    return pl.pallas_call(
        _ffn_kernel,
        out_shape=jax.ShapeDtypeStruct((n_blocks * tm * TOKEN_ROWS, LANES), U32),
        grid_spec=pltpu.PrefetchScalarGridSpec(
            num_scalar_prefetch=2,
            grid=(n_blocks,),
            in_specs=[
                tile,
                pl.BlockSpec((1, d, ff), lambda i, be, on: (be[i], 0, 0)),
                pl.BlockSpec((1, d, ff), lambda i, be, on: (be[i], 0, 0)),
                pl.BlockSpec((1, ff, d), lambda i, be, on: (be[i], 0, 0)),
            ],
            out_specs=tile,
            scratch_shapes=[pltpu.VMEM((d, 2 * ff), BF16), pltpu.VMEM((ff, d), BF16)],
        ),
        compiler_params=_cparams("arbitrary"),
        name="moe_ffn",
    )(blk_e, blk_on, xs, w1, w3, w2)


def _combine_kernel(dest_ref, dnext_ref, gate_ref, sh_ref, h_ref, gf_ref, g_ref, b_ref, ys_ref, o_ref, buf_ref, sem):
    tm = h_ref.shape[0]
    i = pl.program_id(0)
    slot = i % 2

    def copy(sl, r, j, d):
        return pltpu.make_async_copy(_token_tile(ys_ref, d), _token_tile(buf_ref.at[sl, j], r), sem.at[sl])

    def start_tile(idx_ref, sl):
        def start(r, carry):
            for j in range(TOP_K):
                copy(sl, r, j, idx_ref[0, 0, r * TOP_K + j]).start(priority=j % 2)
            return carry

        lax.fori_loop(0, tm, start, 0, unroll=2)

    @pl.when(i == 0)
    def _():
        start_tile(dest_ref, 0)

    @pl.when(i + 1 < pl.num_programs(0))
    def _():
        start_tile(dnext_ref, 1 - slot)

    pltpu.make_async_copy(buf_ref.at[slot], buf_ref.at[slot], sem.at[slot]).wait()

    f = _unpack_pair(_load_tokens(sh_ref, tm)).astype(F32)
    gate = gate_ref[...]
    for j in range(TOP_K):
        f = f + gate[:, j:j + 1] * _unpack_pair(_load_tokens(buf_ref.at[slot, j], tm)).astype(F32)
    o_ref[...] = _layer_norm(DEEPNORM_ALPHA * h_ref[...] + gf_ref[0] * f, g_ref[...], b_ref[...])


def _combine(ys, dest, gate, shared, h, mod3, ln_g, ln_b, dims):
    n, d = h.shape
    tm = COMBINE_TILE
    seq = _seq_of_tile(dims, tm)
    dest3 = dest.reshape(n // tm, 1, tm * TOP_K)
    return pl.pallas_call(
        _combine_kernel,
        out_shape=jax.ShapeDtypeStruct((n, d), F32),
        grid=(n // tm,),
        in_specs=[
            pl.BlockSpec((1, 1, tm * TOP_K), lambda i: (i, 0, 0), memory_space=pltpu.SMEM),
            pl.BlockSpec((1, 1, tm * TOP_K), lambda i: (jnp.minimum(i + 1, n // tm - 1), 0, 0),
                         memory_space=pltpu.SMEM),
            pl.BlockSpec((tm, LANES), lambda i: (i, 0)),
            pl.BlockSpec((tm * TOKEN_ROWS, LANES), lambda i: (i, 0)),
            pl.BlockSpec((tm, d), lambda i: (i, 0)),
            pl.BlockSpec((1, 1, d), lambda i: (seq(i) * 6 + 5, 0, 0)),
            pl.BlockSpec((1, d), lambda i: (0, 0)),
            pl.BlockSpec((1, d), lambda i: (0, 0)),
            pl.BlockSpec(memory_space=pl.ANY),
        ],
        out_specs=pl.BlockSpec((tm, d), lambda i: (i, 0)),
        scratch_shapes=[pltpu.VMEM((2, TOP_K, tm * TOKEN_ROWS, LANES), U32), pltpu.SemaphoreType.DMA((2,))],
        compiler_params=_cparams("arbitrary"),
        name="moe_combine_ln",
    )(dest3, dest3, gate, shared, h, mod3, ln_g.reshape(1, d), ln_b.reshape(1, d), ys)


def _moe(vfp, h, mod3, router_w, bias, w1, w3, w2, sw1, sw3, sw2, layer, ln_g, ln_b, dims, xs_buf):
    n = vfp.shape[0] // TOKEN_ROWS
    tmb = MOE_BLOCK
    idx, gate, rank, cnt = _router(vfp, router_w, bias)
    counts = cnt[0, :N_EXPERTS]
    padded = (counts + tmb - 1) // tmb * tmb
    pend = jnp.cumsum(padded)
    pstart = pend - padded
    n_blocks = (n * TOP_K + N_EXPERTS * (tmb - 1) + tmb - 1) // tmb
    blk_pos = jnp.arange(n_blocks, dtype=jnp.int32) * tmb
    blk_e = jnp.minimum(jnp.sum((pend[None, :] <= blk_pos[:, None]).astype(jnp.int32), axis=1), N_EXPERTS - 1)
    blk_on = (blk_pos < pend[-1]).astype(jnp.int32)
    e_sel = idx[:, :TOP_K]
    onehot = e_sel[:, :, None] == jnp.arange(N_EXPERTS, dtype=jnp.int32)[None, None, :]
    dest = rank[:, :TOP_K] + jnp.sum(jnp.where(onehot, pstart[None, None, :], 0), axis=-1)
    dest = dest.reshape(-1).astype(jnp.int32)
    if xs_buf is None:
        xs_buf = jnp.zeros((n_blocks * tmb * TOKEN_ROWS, LANES), U32)
    xs = _dispatch(vfp, dest, xs_buf)
    flat = lambda w: w.reshape((-1,) + w.shape[-2:])
    ys = _grouped_ffn(xs, blk_e + layer * N_EXPERTS, blk_on, flat(w1), flat(w3), flat(w2), tmb)
    ones = jnp.ones((n // tmb,), jnp.int32)
    shared = _grouped_ffn(vfp, jnp.full((n // tmb,), layer, jnp.int32), ones, sw1, sw3, sw2, tmb)
    return _combine(ys, dest, gate, shared, h, mod3, ln_g, ln_b, dims), xs


def _to_scan(rows, b):
    t = rows.shape[0] // b
    return rows.reshape(b, t, RW_HEADS, RW_HEAD).transpose(1, 3, 0, 2).reshape(t, RW_HEAD, b * RW_HEADS)


def _from_scan(y, b):
    t = y.shape[0]
    return y.reshape(t, RW_HEAD, b, RW_HEADS).transpose(2, 0, 3, 1).reshape(b * t, RW_WIDTH)


def _even_params(j, even_w_in, rw_mu, rw_w0, rw_w_up, rw_a0, rw_a_up, rw_g_up, rw_kk, rw_ka, rw_rk, rw_gn_g,
                 rw_gn_b, cv_w, cv_b, cv_ln_g, cv_ln_b):
    w3 = 3 * RW_WIDTH
    o_g, o_w, o_a = w3, w3 + GATE_LORA, w3 + GATE_LORA + 2 * DECAY_LORA
    o_cv = o_a + 2 * ICLR_LORA

    def relay(m):
        lead = m.shape[0]
        z = lambda k: jnp.zeros((lead, k), m.dtype)
        pieces = [m[:, :w3], m[:, o_g:o_g + GATE_LORA], z(LANES - GATE_LORA)]
        for d in range(2):
            pieces += [m[:, o_w + d * DECAY_LORA:o_w + (d + 1) * DECAY_LORA], z(LANES - DECAY_LORA)]
        for d in range(2):
            pieces += [m[:, o_a + d * ICLR_LORA:o_a + (d + 1) * ICLR_LORA], z(LANES - ICLR_LORA)]
        return jnp.concatenate(pieces, axis=1)

    w_in = even_w_in[j]
    w_pad = jnp.concatenate([relay(w_in), jnp.zeros((w_in.shape[0], 2 * 2048 - RW_STREAM_PAD), w_in.dtype),
                             w_in[:, o_cv:]], axis=1).astype(BF16)
    pad_rows = lambda m, k: jnp.concatenate([m, jnp.zeros((k - m.shape[0], m.shape[1]), m.dtype)], axis=0)
    head = jnp.arange(RW_WIDTH) // RW_HEAD
    vec = lambda m: m.reshape(1, -1)
    return {
        "w_in": w_pad,
        "mu": relay(rw_mu[j]),
        "w0": rw_w0[j], "a0": rw_a0[j],
        "w_up": jnp.stack([pad_rows(rw_w_up[j, d], LANES) for d in range(2)]).astype(BF16),
        "a_up": jnp.stack([pad_rows(rw_a_up[j, d], LANES) for d in range(2)]).astype(BF16),
        "g_up": pad_rows(rw_g_up[j], LANES).astype(BF16),
        "kk": vec(rw_kk[j]), "ka": vec(rw_ka[j]), "rk": vec(rw_rk[j]),
        "hs": (head[:, None] == head[None, :]).astype(BF16),
        "gn_g": vec(rw_gn_g[j]), "gn_b": vec(rw_gn_b[j]),
        "cv_w": pad_rows(cv_w[j], 32), "cv_b": vec(cv_b[j]), "cv_g": vec(cv_ln_g[j]), "cv_beta": vec(cv_ln_b[j]),
    }


def _even_mixer(h, mod3, prm, dims):
    p = _inproj(h, mod3, prm["w_in"], dims, 1, 0)
    r, v, kk, g, bon, dec, bvec, kd = _rwkv_features(p, prm, dims)
    b = dims[0]
    nl = b * dims[1]
    state = [jnp.zeros((RW_HEAD, RW_HEAD, LANES), F32)] * 2
    y_parts = []
    for rows in (slice(nl, None), slice(0, nl)):
        shared = [_to_scan(x[rows], b) for x in (r, kk, v)]
        y_sum = None
        for d in range(2):
            per_dir = [_to_scan(x[d, rows], b) for x in (dec, bvec, kd)]
            y_sum, state[d] = _wkv_scan(*shared, *per_dir, state[d], reverse=(d == 1), y_prev=y_sum)
        y_parts.append(_from_scan(y_sum, b))
    rw = _rwkv_post(jnp.concatenate([y_parts[1], y_parts[0]], axis=0), bon, g, prm)
    cv = _conformer_conv(p, prm, dims)
    return [rw, cv]


def _attn_mixer(h, mod3, w_in, qn, kn, tables, dims, with_ctx):
    p = _inproj(h, mod3, w_in, dims, 1, 0)
    q, k, v = _qkv_post(p, tables[0], tables[1], qn, kn, dims)
    out = _attention(q, k, v, dims, ctx_queries=False)
    if not with_ctx:
        return [out]
    return [jnp.concatenate([out, _attention(q, k, v, dims, ctx_queries=True)], axis=0)]


def kernel(x, c, ctx, c_ctx, ada_w, ada_b, ln1_g, ln1_b, ln2_g, ln2_b, even_w_in, even_w_out, rw_mu, rw_w0, rw_w_up, rw_a0, rw_a_up, rw_g_up, rw_kk, rw_ka, rw_rk, rw_gn_g, rw_gn_b, cv_w, cv_b, cv_ln_g, cv_ln_b, odd_w_in, odd_w_out, q_norm, k_norm, moe_router, moe_bias, moe_w1, moe_w3, moe_w2, sh_w1, sh_w3, sh_w2):
    bsz, s_len, d = x.shape
    ctx_len = ctx.shape[1]
    dims = (bsz, s_len, ctx_len)
    assert bsz * RW_HEADS == LANES, "the WKV scan maps (batch, head) pairs onto the 128 lanes"
    n_lat = bsz * s_len
    depth = ada_w.shape[0]

    c16 = jnp.zeros((16, d), F32).at[:bsz].set(c).at[bsz].set(c_ctx)
    mod_all = _ada_all(c16, ada_w, ada_b)[:, :bsz + 1].reshape(depth, (bsz + 1) * 6, 1, d)
    tables = _rope_tables(s_len, min(ROW_TILE, ctx_len))

    h = jnp.concatenate([x.reshape(n_lat, d), ctx.reshape(bsz * ctx_len, d)], axis=0)
    xs_buf = None
    for layer in range(depth):
        j = layer // 2
        last = layer == depth - 1
        mod3 = mod_all[layer]
        if layer % 2 == 0:
            prm = _even_params(j, even_w_in, rw_mu, rw_w0, rw_w_up, rw_a0, rw_a_up, rw_g_up, rw_kk, rw_ka, rw_rk,
                               rw_gn_g, rw_gn_b, cv_w, cv_b, cv_ln_g, cv_ln_b)
            acts = _even_mixer(h, mod3, prm, dims)
            w_out = even_w_out[j].astype(BF16)
        else:
            acts = _attn_mixer(h, mod3, odd_w_in[j].astype(BF16), q_norm[j], k_norm[j], tables, dims,
                               with_ctx=not last)
            w_out = odd_w_out[j].astype(BF16)
        h, vfp = _outproj(acts, w_out, h, mod3, ln1_g[layer], ln1_b[layer], dims)
        h, xs_buf = _moe(vfp, h, mod3, moe_router[layer], moe_bias[layer], moe_w1, moe_w3, moe_w2, sh_w1, sh_w3,
                         sh_w2, layer, ln2_g[layer], ln2_b[layer], dims, xs_buf)
    return h[:n_lat].reshape(bsz, s_len, d)
```
